```python
import math
import jax, jax.numpy as jnp
from jax import lax
import numpy as np

D_MODEL = 2048
BATCH = 16
SEQ = 2048
DEPTH = 2

D_FF = 256 * ((8 * D_MODEL // 3 + 255) // 256)
FFN_RES = 0.5
N_MOD = 9
EPS = 1e-6

GDN_HEADS = 8
GDN_HEAD_DIM = 128
GDN_WIDTH = GDN_HEADS * GDN_HEAD_DIM
GDN_CONV = 4
GDN_CHUNK = 64

S5_GROUP = 16
S5_WIDTH = 768
S5_GROUPS = S5_WIDTH // S5_GROUP
S5_STATE = 64
S5_MAX_RE = -1e-4

DIL_PAIRS = ((128, 1), (512, 4), (2048, 16))
DIL_HEADS_PER_GROUP = 4
DIL_HEAD_DIM = 64
DIL_SUBHEADS = len(DIL_PAIRS) * DIL_HEADS_PER_GROUP
DIL_WIDTH = DIL_SUBHEADS * DIL_HEAD_DIM
DIL_OUT = DIL_HEADS_PER_GROUP * DIL_HEAD_DIM
ALIBI_MAX = 8.0

N_BRANCH = 3
IN_SPLITS = (3 * GDN_WIDTH, GDN_WIDTH, GDN_HEADS, GDN_HEADS, S5_WIDTH, 3 * DIL_WIDTH, N_BRANCH * D_MODEL)
IN_COLS = sum(IN_SPLITS)

kernel_name = 'hybrid_gdn_s5_dilated_macaron_block'


def rms_norm(x, g):
    xf = x.astype(jnp.float32)
    y = xf * lax.rsqrt(jnp.mean(xf * xf, axis=-1, keepdims=True) + EPS)
    return (y * g.astype(jnp.float32)).astype(x.dtype)


def l2_normalize(x):
    return x * lax.rsqrt(jnp.sum(x * x, axis=-1, keepdims=True) + EPS)


def modulate(x, shift, scale):
    return x * (1.0 + scale[:, None, :]) + shift[:, None, :]


def swiglu(x, w1, w3, w2):
    return (jax.nn.silu(x @ w1) * (x @ w3)) @ w2


def causal_depthwise_conv(x, w):
    K, C = w.shape
    return lax.conv_general_dilated(x, w[:, None, :].astype(x.dtype), window_strides=(1,),
                                    padding=[(K - 1, 0)], dimension_numbers=('NWC', 'WIO', 'NWC'),
                                    feature_group_count=C)


def gated_delta_rule_chunked(q, k, v, g, beta):
    Bsz, S_, H, Dk = q.shape
    Dv = v.shape[-1]
    C = GDN_CHUNK
    N = S_ // C

    def chunks(t):
        t = t.reshape((Bsz, N, C) + t.shape[2:])
        return jnp.moveaxis(t, 3, 1)

    q, k, v, g, beta = chunks(q), chunks(k), chunks(v), chunks(g), chunks(beta)
    g = jnp.cumsum(g, axis=-1)
    kb = k * beta[..., None]
    vb = v * beta[..., None]
    tril = jnp.tril(jnp.ones((C, C), bool))
    tril_strict = jnp.tril(jnp.ones((C, C), bool), -1)
    decay = jnp.exp(jnp.where(tril, g[..., :, None] - g[..., None, :], -jnp.inf))
    lower = jnp.where(tril_strict, jnp.einsum('bhnid,bhnjd->bhnij', kb, k) * decay, 0.0)
    system = jnp.eye(C, dtype=q.dtype) + lower
    u = lax.linalg.triangular_solve(system, vb, left_side=True, lower=True, unit_diagonal=True)
    w = lax.linalg.triangular_solve(system, kb * jnp.exp(g)[..., None], left_side=True,
                                    lower=True, unit_diagonal=True)

    def step(h, inp):
        qc, kc, uc, wc, gc, dc = inp
        attn = jnp.einsum('bhid,bhjd->bhij', qc, kc) * dc
        v_new = uc - jnp.einsum('bhcd,bhde->bhce', wc, h)
        o = (jnp.einsum('bhcd,bhde->bhce', qc * jnp.exp(gc)[..., None], h)
             + jnp.einsum('bhij,bhje->bhie', attn, v_new))
        g_last = gc[..., -1]
        h = (h * jnp.exp(g_last)[..., None, None]
             + jnp.einsum('bhcd,bhce->bhde', kc * jnp.exp(g_last[..., None] - gc)[..., None], v_new))
        return h, o

    xs = tuple(jnp.moveaxis(t, 2, 0) for t in (q, k, u, w, g, decay))
    h0 = jnp.zeros((Bsz, H, Dk, Dv), q.dtype)
    _, o = lax.scan(step, h0, xs)
    return o.transpose(1, 0, 3, 2, 4).reshape(Bsz, S_, H, Dv)


def gdn_branch(qkv, z, beta_logit, alpha_logit, conv_w, a_log, dt_bias, out_norm):
    Bsz, S_, _ = qkv.shape
    f32 = jnp.float32
    qkv = jax.nn.silu(causal_depthwise_conv(qkv, conv_w)).astype(f32)
    q, k, v = [t.reshape(Bsz, S_, GDN_HEADS, GDN_HEAD_DIM) for t in jnp.split(qkv, 3, axis=-1)]
    q = l2_normalize(q) * GDN_HEAD_DIM ** -0.5
    k = l2_normalize(k)
    beta = jax.nn.sigmoid(beta_logit.astype(f32))
    g = -jnp.exp(a_log.astype(f32)) * jax.nn.softplus(alpha_logit.astype(f32) + dt_bias.astype(f32))
    o = gated_delta_rule_chunked(q, k, v, g, beta)
    o = rms_norm(o, out_norm) * jax.nn.silu(z.astype(f32).reshape(Bsz, S_, GDN_HEADS, GDN_HEAD_DIM))
    return o.reshape(Bsz, S_, GDN_WIDTH).astype(z.dtype)


def s5_branch(u, a_re, a_im, b_re, b_im, c_re, c_im, d_skip, log_step, glu_w, glu_b):
    Bsz, S_, _ = u.shape
    f32 = jnp.float32
    uf = u.astype(f32).reshape(Bsz, S_, S5_GROUPS, S5_GROUP)
    lam = lax.complex(jnp.minimum(a_re.astype(f32), S5_MAX_RE), a_im.astype(f32))
    step = jnp.exp(log_step.astype(f32))[:, None]
    lam_bar = jnp.exp(lam * step)
    b = lax.complex(b_re.astype(f32), b_im.astype(f32))
    b_bar = ((lam_bar - 1.0) / lam)[..., None] * b
    bu = jnp.einsum('gpi,bsgi->bsgp', b_bar, uf)
    a = jnp.broadcast_to(lam_bar, (1, S_) + lam_bar.shape)

    def combine(left, right):
        a_l, b_l = left
        a_r, b_r = right
        return a_r * a_l, a_r * b_l + b_r

    _, states = lax.associative_scan(combine, (a, bu), axis=1)
    cmat = lax.complex(c_re.astype(f32), c_im.astype(f32))
    y = jnp.real(jnp.einsum('gip,bsgp->bsgi', cmat, states)) + d_skip.astype(f32).reshape(S5_GROUPS, S5_GROUP) * uf
    y = jax.nn.gelu(y.reshape(Bsz, S_, S5_WIDTH))
    val, gate = jnp.split(y @ glu_w.astype(f32) + glu_b.astype(f32), 2, axis=-1)
    return (val * jax.nn.sigmoid(gate)).astype(u.dtype)


def dilated_window_attention(q, k, v, slopes, window, dilation):
    Bsz, S_, H, E = q.shape
    span = window // dilation
    sd = S_ // dilation
    blk = min(span, sd)
    nb = -(-sd // blk)
    sp = nb * blk

    def residues(t):
        t = t.reshape(Bsz, sd, dilation, H, E).transpose(0, 2, 1, 3, 4)
        return jnp.pad(t, ((0, 0), (0, 0), (0, sp - sd), (0, 0), (0, 0)))

    def key_blocks(t):
        t = jnp.pad(t, ((0, 0), (0, 0), (blk, 0), (0, 0), (0, 0))).reshape(Bsz, dilation, nb + 1, blk, H, E)
        return jnp.concatenate([t[:, :, :-1], t[:, :, 1:]], axis=3)

    qb = residues(q).reshape(Bsz, dilation, nb, blk, H, E)
    kb = key_blocks(residues(k))
    vb = key_blocks(residues(v))
    s = jnp.einsum('brnqhe,brnkhe->brnhqk', qb, kb)
    qi = jnp.arange(blk)[:, None]
    kj = jnp.arange(2 * blk)[None, :]
    steps = qi - kj + blk
    key_pos = jnp.arange(nb)[:, None, None] * blk + kj[None] - blk
    valid = (steps >= 0) & (steps <= span) & (key_pos >= 0)
    bias = -slopes[:, None, None] * (steps * dilation).astype(jnp.float32)[None]
    s = jnp.where(valid[None, None, :, None], s + bias[None, None, None], -jnp.inf)
    m = jnp.max(s, axis=-1, keepdims=True)
    p = jnp.exp(s - m)
    l = jnp.sum(p, axis=-1, keepdims=True)
    o = jnp.einsum('brnhqk,brnkhe->brnqhe', p / l, vb)
    lse = (m + jnp.log(l))[..., 0].transpose(0, 1, 2, 4, 3)

    def back(t):
        t = t.reshape((Bsz, dilation, sp) + t.shape[4:])[:, :, :sd]
        t = jnp.moveaxis(t, 1, 2)
        return t.reshape((Bsz, S_) + t.shape[3:])

    return back(o), back(lse)


def dilated_branch(qkv, q_norm, k_norm):
    Bsz, S_, _ = qkv.shape
    f32 = jnp.float32
    shape = (Bsz, S_, DIL_SUBHEADS, DIL_HEAD_DIM)
    q, k, v = [t.astype(f32).reshape(shape) for t in jnp.split(qkv, 3, axis=-1)]
    q = rms_norm(q, q_norm) * DIL_HEAD_DIM ** -0.5
    k = rms_norm(k, k_norm)
    slopes = jnp.power(2.0, -ALIBI_MAX * jnp.arange(1, DIL_SUBHEADS + 1, dtype=f32) / DIL_SUBHEADS)
    outs, lses = [], []
    for gi, (window, dilation) in enumerate(DIL_PAIRS):
        hs = slice(gi * DIL_HEADS_PER_GROUP, (gi + 1) * DIL_HEADS_PER_GROUP)
        o, lse = dilated_window_attention(q[:, :, hs], k[:, :, hs], v[:, :, hs], slopes[hs], window, dilation)
        outs.append(o)
        lses.append(lse)
    weights = jax.nn.softmax(jnp.stack(lses, 0), axis=0)
    o = jnp.sum(weights[..., None] * jnp.stack(outs, 0), axis=0)
    return o.reshape(Bsz, S_, DIL_OUT).astype(qkv.dtype)


def hybrid_mixer(u, w_in, gdn_conv, gdn_a_log, gdn_dt_bias, gdn_out_norm,
                 s5_a_re, s5_a_im, s5_b_re, s5_b_im, s5_c_re, s5_c_im, s5_d, s5_log_step, s5_glu_w, s5_glu_b,
                 dil_q_norm, dil_k_norm, w_branch_a, w_branch_b, w_branch_c, w_out):
    Bsz, S_, D = u.shape
    proj = u @ w_in
    offsets = np.cumsum(IN_SPLITS)[:-1].tolist()
    a_qkv, a_z, a_beta, a_alpha, b_u, c_qkv, gate_logits = jnp.split(proj, offsets, axis=-1)
    y_a = gdn_branch(a_qkv, a_z, a_beta, a_alpha, gdn_conv, gdn_a_log, gdn_dt_bias, gdn_out_norm)
    y_b = s5_branch(b_u, s5_a_re, s5_a_im, s5_b_re, s5_b_im, s5_c_re, s5_c_im, s5_d, s5_log_step, s5_glu_w, s5_glu_b)
    y_c = dilated_branch(c_qkv, dil_q_norm, dil_k_norm)
    gates = jax.nn.sigmoid(gate_logits.astype(jnp.float32)).astype(u.dtype).reshape(Bsz, S_, N_BRANCH, D)
    merged = (gates[:, :, 0] * (y_a @ w_branch_a)
              + gates[:, :, 1] * (y_b @ w_branch_b)
              + gates[:, :, 2] * (y_c @ w_branch_c))
    return merged @ w_out


def setup_inputs(seed: int = 0) -> dict:
    key = jax.random.key(seed)
    keys = list(jax.random.split(key, 48))
    f32 = jnp.float32
    L, D = DEPTH, D_MODEL
    G, P, I = S5_GROUPS, S5_STATE, S5_GROUP

    def nrm(shape, scale):
        return jax.random.normal(keys.pop(), shape, f32) * scale

    def gain(shape):
        return 1.0 + nrm(shape, 0.05)

    def unif(shape, lo, hi):
        return jax.random.uniform(keys.pop(), shape, f32, lo, hi)

    dt = jnp.exp(unif((L, GDN_HEADS), math.log(1e-3), math.log(1e-1)))
    return {
        'x': nrm((BATCH, SEQ, D), 1.0),
        'c': nrm((BATCH, D), 1.0),
        'ada_w': nrm((L, D, N_MOD * D), 0.5 * D ** -0.5),
        'ada_b': nrm((L, N_MOD * D), 0.01),
        'norm_ffn1': gain((L, D)),
        'ffn1_w1': nrm((L, D, D_FF), D ** -0.5),
        'ffn1_w3': nrm((L, D, D_FF), D ** -0.5),
        'ffn1_w2': nrm((L, D_FF, D), D_FF ** -0.5),
        'norm_mix': gain((L, D)),
        'w_in': nrm((L, D, IN_COLS), D ** -0.5),
        'gdn_conv': nrm((L, GDN_CONV, 3 * GDN_WIDTH), GDN_CONV ** -0.5),
        'gdn_a_log': jnp.log(unif((L, GDN_HEADS), 1.0, 16.0)),
        'gdn_dt_bias': dt + jnp.log(-jnp.expm1(-dt)),
        'gdn_out_norm': gain((L, GDN_HEAD_DIM)),
        's5_a_re': -0.5 + nrm((L, G, P), 0.01),
        's5_a_im': math.pi * jnp.arange(P, dtype=f32) + nrm((L, G, P), 0.01),
        's5_b_re': nrm((L, G, P, I), (2 * I) ** -0.5),
        's5_b_im': nrm((L, G, P, I), (2 * I) ** -0.5),
        's5_c_re': nrm((L, G, I, P), (2 * P) ** -0.5),
        's5_c_im': nrm((L, G, I, P), (2 * P) ** -0.5),
        's5_d': nrm((L, S5_WIDTH), 1.0),
        's5_log_step': unif((L, G), math.log(1e-3), math.log(1e-1)),
        's5_glu_w': nrm((L, S5_WIDTH, 2 * S5_WIDTH), S5_WIDTH ** -0.5),
        's5_glu_b': nrm((L, 2 * S5_WIDTH), 0.01),
        'dil_q_norm': gain((L, DIL_HEAD_DIM)),
        'dil_k_norm': gain((L, DIL_HEAD_DIM)),
        'w_branch_a': nrm((L, GDN_WIDTH, D), GDN_WIDTH ** -0.5),
        'w_branch_b': nrm((L, S5_WIDTH, D), S5_WIDTH ** -0.5),
        'w_branch_c': nrm((L, DIL_OUT, D), DIL_OUT ** -0.5),
        'w_out': nrm((L, D, D), D ** -0.5),
        'norm_ffn2': gain((L, D)),
        'ffn2_w1': nrm((L, D, D_FF), D ** -0.5),
        'ffn2_w3': nrm((L, D, D_FF), D ** -0.5),
        'ffn2_w2': nrm((L, D_FF, D), D_FF ** -0.5),
    }


def reference(x, c, ada_w, ada_b, norm_ffn1, ffn1_w1, ffn1_w3, ffn1_w2, norm_mix, w_in,
              gdn_conv, gdn_a_log, gdn_dt_bias, gdn_out_norm,
              s5_a_re, s5_a_im, s5_b_re, s5_b_im, s5_c_re, s5_c_im, s5_d, s5_log_step, s5_glu_w, s5_glu_b,
              dil_q_norm, dil_k_norm, w_branch_a, w_branch_b, w_branch_c, w_out,
              norm_ffn2, ffn2_w1, ffn2_w3, ffn2_w2):
    for l in range(DEPTH):
        mod = jax.nn.silu(c) @ ada_w[l] + ada_b[l]
        sh1, sc1, g1, sh2, sc2, g2, sh3, sc3, g3 = jnp.split(mod, N_MOD, axis=-1)
        h = modulate(rms_norm(x, norm_ffn1[l]), sh1, sc1)
        x = x + FFN_RES * g1[:, None, :] * swiglu(h, ffn1_w1[l], ffn1_w3[l], ffn1_w2[l])
        h = modulate(rms_norm(x, norm_mix[l]), sh2, sc2)
        x = x + g2[:, None, :] * hybrid_mixer(
            h, w_in[l], gdn_conv[l], gdn_a_log[l], gdn_dt_bias[l], gdn_out_norm[l],
            s5_a_re[l], s5_a_im[l], s5_b_re[l], s5_b_im[l], s5_c_re[l], s5_c_im[l], s5_d[l], s5_log_step[l],
            s5_glu_w[l], s5_glu_b[l], dil_q_norm[l], dil_k_norm[l],
            w_branch_a[l], w_branch_b[l], w_branch_c[l], w_out[l])
        h = modulate(rms_norm(x, norm_ffn2[l]), sh3, sc3)
        x = x + FFN_RES * g3[:, None, :] * swiglu(h, ffn2_w1[l], ffn2_w3[l], ffn2_w2[l])
    return x
```

```python
import functools
import math

import jax
import jax.numpy as jnp
from jax import lax
from jax.experimental import pallas as pl
from jax.experimental.pallas import tpu as pltpu

F32 = jnp.float32
BF16 = jnp.bfloat16
EPS = 1e-6
NEG = -1e30

LANES = 128
V7X_VMEM_LIMIT_BYTES = 56 * 1024 * 1024

FFN_RES = 0.5
N_MOD = 9
GDN_HEADS = 8
GDN_HEAD_DIM = 128
GDN_WIDTH = GDN_HEADS * GDN_HEAD_DIM
GDN_CONV = 4
GDN_CHUNK = 64
S5_GROUP = 16
S5_WIDTH = 768
S5_GROUPS = S5_WIDTH // S5_GROUP
S5_STATE = 64
S5_MAX_RE = -1e-4
S5_TILE_GROUPS = LANES // S5_GROUP
S5_TILES = S5_GROUPS // S5_TILE_GROUPS
S5_TILE_STATES = S5_TILE_GROUPS * S5_STATE
DIL_PAIRS = ((128, 1), (512, 4), (2048, 16))
DIL_HEADS_PER_GROUP = 4
DIL_HEAD_DIM = 64
DIL_SUBHEADS = len(DIL_PAIRS) * DIL_HEADS_PER_GROUP
DIL_WIDTH = DIL_SUBHEADS * DIL_HEAD_DIM
DIL_OUT = DIL_HEADS_PER_GROUP * DIL_HEAD_DIM
DIL_BLK = 128
ALIBI_MAX = 8.0
N_BRANCH = 3

OFF_S5 = 0
OFF_CQ = OFF_S5 + S5_WIDTH
OFF_CK = OFF_CQ + DIL_WIDTH
OFF_CV = OFF_CK + DIL_WIDTH
OFF_Z = OFF_CV + DIL_WIDTH
OFF_GATES = OFF_Z + GDN_WIDTH
OFF_AQ = None


def _cparams(sem, vmem=V7X_VMEM_LIMIT_BYTES):
    return pltpu.CompilerParams(dimension_semantics=sem, vmem_limit_bytes=vmem)


def _dot(a, b, precision=None):
    return jnp.dot(a, b, preferred_element_type=F32, precision=precision)


def _dot_nt(a, b, precision=None):
    return lax.dot_general(a, b, (((1,), (1,)), ((), ())), preferred_element_type=F32,
                           precision=precision)


def _dot_tn(a, b, precision=None):
    return lax.dot_general(a, b, (((0,), (0,)), ((), ())), preferred_element_type=F32,
                           precision=precision)


def _silu(x):
    return x * jax.nn.sigmoid(x)


def _ada_kernel(c_ref, w_ref, b_ref, o_ref):
    c = c_ref[...]
    o_ref[0] = _dot(_silu(c), w_ref[0]) + b_ref[0]


def _ada_mod(c, ada_w, ada_b, tn=1024):
    L, D, N = ada_w.shape
    B = c.shape[0]
    return pl.pallas_call(
        _ada_kernel,
        grid=(L, N // tn),
        in_specs=[pl.BlockSpec((B, D), lambda l, j: (0, 0)),
                  pl.BlockSpec((1, D, tn), lambda l, j: (l, 0, j)),
                  pl.BlockSpec((1, 1, tn), lambda l, j: (l, 0, j))],
        out_specs=pl.BlockSpec((1, B, tn), lambda l, j: (l, 0, j)),
        out_shape=jax.ShapeDtypeStruct((L, B, N), F32),
        compiler_params=_cparams(("arbitrary", "arbitrary")),
        name="ada_mod",
    )(c, ada_w, ada_b.reshape(L, 1, N))


def _norm_mod(x, g, sh, sc):
    ms = jnp.mean(x * x, axis=-1, keepdims=True)
    y = x * lax.rsqrt(ms + EPS) * g
    return y * (1.0 + sc) + sh


def _ffn_kernel(x_ref, g_ref, sh_ref, sc_ref, gate_ref, w1_ref, w3_ref, w2_ref, o_ref,
                hn_ref, acc_ref):
    j = pl.program_id(1)

    @pl.when(j == 0)
    def _():
        hn_ref[...] = _norm_mod(x_ref[0], g_ref[...], sh_ref[0], sc_ref[0]).astype(BF16)
        acc_ref[...] = jnp.zeros_like(acc_ref)

    hn = hn_ref[...]
    h1 = _dot(hn, w1_ref[...])
    h3 = _dot(hn, w3_ref[...])
    a = (_silu(h1) * h3).astype(BF16)
    acc_ref[...] += _dot(a, w2_ref[...])

    @pl.when(j == pl.num_programs(1) - 1)
    def _():
        o_ref[0] = x_ref[0] + FFN_RES * gate_ref[0] * acc_ref[...]


def _ffn(x, g, sh, sc, gate, w1, w3, w2, tm=512, tf=512):
    B, S, D = x.shape
    F = w1.shape[1]
    nS = S // tm
    xmap = lambda i, j: (i // nS, i % nS, 0)
    bmap = lambda i, j: (i // nS, 0, 0)
    return pl.pallas_call(
        _ffn_kernel,
        grid=(B * nS, F // tf),
        in_specs=[pl.BlockSpec((1, tm, D), xmap),
                  pl.BlockSpec((1, D), lambda i, j: (0, 0)),
                  pl.BlockSpec((1, 1, D), bmap),
                  pl.BlockSpec((1, 1, D), bmap),
                  pl.BlockSpec((1, 1, D), bmap),
                  pl.BlockSpec((D, tf), lambda i, j: (0, j)),
                  pl.BlockSpec((D, tf), lambda i, j: (0, j)),
                  pl.BlockSpec((tf, D), lambda i, j: (j, 0))],
        out_specs=pl.BlockSpec((1, tm, D), xmap),
        out_shape=jax.ShapeDtypeStruct((B, S, D), F32),
        scratch_shapes=[pltpu.VMEM((tm, D), BF16), pltpu.VMEM((tm, D), F32)],
        compiler_params=_cparams(("arbitrary", "arbitrary")),
        name="ffn",
    )(x, g.reshape(1, D), sh.reshape(B, 1, D), sc.reshape(B, 1, D), gate.reshape(B, 1, D),
      w1, w3, w2)


def _proj_kernel(x_ref, g_ref, sh_ref, sc_ref, w_ref, o_ref, hn_ref):
    @pl.when(pl.program_id(1) == 0)
    def _():
        hn_ref[...] = _norm_mod(x_ref[0], g_ref[...], sh_ref[0], sc_ref[0]).astype(BF16)

    o_ref[0] = _dot(hn_ref[...], w_ref[...])


def _proj(x, g, sh, sc, w, tm=512, tn=1920):
    B, S, D = x.shape
    N = w.shape[1]
    nS = S // tm
    xmap = lambda i, j: (i // nS, i % nS, 0)
    bmap = lambda i, j: (i // nS, 0, 0)
    return pl.pallas_call(
        _proj_kernel,
        grid=(B * nS, N // tn),
        in_specs=[pl.BlockSpec((1, tm, D), xmap),
                  pl.BlockSpec((1, D), lambda i, j: (0, 0)),
                  pl.BlockSpec((1, 1, D), bmap),
                  pl.BlockSpec((1, 1, D), bmap),
                  pl.BlockSpec((D, tn), lambda i, j: (0, j))],
        out_specs=pl.BlockSpec((1, tm, tn), lambda i, j: (i // nS, i % nS, j)),
        out_shape=jax.ShapeDtypeStruct((B, S, N), F32),
        scratch_shapes=[pltpu.VMEM((tm, D), BF16)],
        compiler_params=_cparams(("arbitrary", "arbitrary")),
        name="in_proj",
    )(x, g.reshape(1, D), sh.reshape(B, 1, D), sc.reshape(B, 1, D), w)


def _gdn_kernel(q_ref, k_ref, v_ref, z_ref, ba_ref, cq_ref, ck_ref, cv_ref, al_ref, dtb_ref,
                on_ref, o_ref, qn_s, kn_s, vv_s, uw_s, attn_s, gc_s, gl_s, be_s, gct_s,
                *, S, HG, solve_precision):
    C = GDN_CHUNK
    HD = GDN_HEAD_DIM
    hg = pl.program_id(1)
    shift = (LANES - hg * HG) % LANES
    ba = pltpu.roll(ba_ref[0], shift, 1)
    a_log = pltpu.roll(al_ref[...], shift, 1)
    dtb = pltpu.roll(dtb_ref[...], shift, 1)
    be_s[...] = jax.nn.sigmoid(ba)
    xg = ba + dtb
    softplus = jnp.maximum(xg, 0.0) + jnp.log1p(jnp.exp(-jnp.abs(xg)))
    g = -jnp.exp(a_log) * softplus
    rowc = lax.broadcasted_iota(jnp.int32, (S, LANES), 0) % C
    for d in (1, 2, 4, 8, 16, 32):
        g = g + jnp.where(rowc >= d, pltpu.roll(g, d, 0), 0.0)
    gc_s[...] = g
    g3 = g.reshape(S // C, C, LANES)
    gl_s[...] = jnp.broadcast_to(g3[:, C - 1:C, :], (S // C, C, LANES)).reshape(S, LANES)
    gct_s[...] = g.T[GDN_HEADS:2 * GDN_HEADS, :]

    rowS = lax.broadcasted_iota(jnp.int32, (S, HD), 0)

    def conv_silu(x_ref, w_ref, sl):
        x = x_ref[0, :, sl]
        w = w_ref[:, sl]
        acc = x * w[GDN_CONV - 1:GDN_CONV, :]
        for kk in range(GDN_CONV - 1):
            s = GDN_CONV - 1 - kk
            acc = acc + jnp.where(rowS >= s, pltpu.roll(x, s, 0), 0.0) * w[kk:kk + 1, :]
        return _silu(acc)

    def l2n(x):
        return x * lax.rsqrt(jnp.sum(x * x, axis=-1, keepdims=True) + EPS)

    for j in range(HG):
        sl = slice(j * HD, (j + 1) * HD)
        qn_s[:, sl] = l2n(conv_silu(q_ref, cq_ref, sl)) * (HD ** -0.5)
        kn_s[:, sl] = l2n(conv_silu(k_ref, ck_ref, sl))
        vv_s[:, sl] = conv_silu(v_ref, cv_ref, sl)

    ii = lax.broadcasted_iota(jnp.int32, (C, C), 0)
    jj = lax.broadcasted_iota(jnp.int32, (C, C), 1)
    tril = ii >= jj
    strict = ii > jj
    sp = solve_precision

    def phase1(p, carry):
        r0 = pl.multiple_of(p * 2 * C, 2 * C)
        gct = gct_s[:, pl.ds(r0, 2 * C)]
        for cc in range(2):
            rows = pl.ds(r0 + cc * C, C)
            gcc = gc_s[rows, :]
            bec = be_s[rows, :]
            for j in range(HG):
                sl = slice(j * HD, (j + 1) * HD)
                gcol = gcc[:, GDN_HEADS + j:GDN_HEADS + j + 1]
                grow = gct[j:j + 1, cc * C:(cc + 1) * C]
                decay = jnp.where(tril, jnp.exp(jnp.minimum(gcol - grow, 0.0)), 0.0)
                bcol = bec[:, j:j + 1]
                kn_c = kn_s[rows, sl]
                qn_c = qn_s[rows, sl]
                kb = kn_c * bcol
                eg = jnp.exp(gcol)
                rhs = jnp.concatenate([vv_s[rows, sl] * bcol, kb * eg], axis=1)
                lm = jnp.where(strict, _dot_nt(kb, kn_c) * decay, 0.0)
                x = rhs - _dot(lm, rhs, sp)
                pw = _dot(lm, lm, sp)
                for it in range(5):
                    x = x + _dot(pw, x, sp)
                    if it < 4:
                        pw = _dot(pw, pw, sp)
                uw_s[rows, j * 2 * HD:(j + 1) * 2 * HD] = x
                attn_s[j, rows, :] = _dot_nt(qn_c, kn_c) * decay
                qn_s[rows, sl] = qn_c * eg
        return carry

    lax.fori_loop(0, S // (2 * C), phase1, 0)

    def phase2(c, hs):
        rows = pl.ds(pl.multiple_of(c * C, C), C)
        gcc = gc_s[rows, :]
        glc = gl_s[rows, :]
        new_hs = []
        for j in range(HG):
            sl = slice(j * HD, (j + 1) * HD)
            h = hs[j]
            gcol = gcc[:, GDN_HEADS + j:GDN_HEADS + j + 1]
            glcol = glc[:, GDN_HEADS + j:GDN_HEADS + j + 1]
            u = uw_s[rows, j * 2 * HD:j * 2 * HD + HD]
            w = uw_s[rows, j * 2 * HD + HD:(j + 1) * 2 * HD]
            v_new = u - _dot(w, h)
            o = _dot(qn_s[rows, sl], h) + _dot(attn_s[j, rows, :], v_new)
            kdec = kn_s[rows, sl] * jnp.exp(glcol - gcol)
            h = h * jnp.exp(glcol[0:1, :]) + _dot_tn(kdec, v_new)
            new_hs.append(h)
            on = o * lax.rsqrt(jnp.mean(o * o, axis=-1, keepdims=True) + EPS) * on_ref[...]
            zc = z_ref[0, rows, sl]
            o_ref[0, rows, sl] = (on * _silu(zc)).astype(o_ref.dtype)
        return tuple(new_hs)

    h0 = tuple(jnp.zeros((HD, HD), F32) for _ in range(HG))
    lax.fori_loop(0, S // C, phase2, h0)


def _gdn(proj, conv_w, a_log, dt_bias, out_norm, off_aq, off_ba, HG=2,
         solve_precision=lax.Precision.HIGHEST):
    B, S, _ = proj.shape
    W = HG * GDN_HEAD_DIM
    nG = GDN_HEADS // HG
    pad = lambda v: jnp.zeros((1, LANES), F32).at[0, GDN_HEADS:2 * GDN_HEADS].set(v)
    kern = functools.partial(_gdn_kernel, S=S, HG=HG, solve_precision=solve_precision)
    col = lambda base: (lambda b, h: (b, 0, base // W + h))
    cw = lambda base: (lambda b, h: (0, base // W + h))
    one = lambda b, h: (0, 0)
    return pl.pallas_call(
        kern,
        grid=(B, nG),
        in_specs=[pl.BlockSpec((1, S, W), col(off_aq)),
                  pl.BlockSpec((1, S, W), col(off_aq + GDN_WIDTH)),
                  pl.BlockSpec((1, S, W), col(off_aq + 2 * GDN_WIDTH)),
                  pl.BlockSpec((1, S, W), col(OFF_Z)),
                  pl.BlockSpec((1, S, LANES), lambda b, h: (b, 0, off_ba // LANES)),
                  pl.BlockSpec((GDN_CONV, W), cw(0)),
                  pl.BlockSpec((GDN_CONV, W), cw(GDN_WIDTH)),
                  pl.BlockSpec((GDN_CONV, W), cw(2 * GDN_WIDTH)),
                  pl.BlockSpec((1, LANES), one),
                  pl.BlockSpec((1, LANES), one),
                  pl.BlockSpec((1, GDN_HEAD_DIM), one)],
        out_specs=pl.BlockSpec((1, S, W), lambda b, h: (b, 0, h)),
        out_shape=jax.ShapeDtypeStruct((B, S, GDN_WIDTH), BF16),
        scratch_shapes=[pltpu.VMEM((S, W), F32), pltpu.VMEM((S, W), F32), pltpu.VMEM((S, W), F32),
                        pltpu.VMEM((S, 2 * W), F32), pltpu.VMEM((HG, S, GDN_CHUNK), F32),
                        pltpu.VMEM((S, LANES), F32), pltpu.VMEM((S, LANES), F32),
                        pltpu.VMEM((S, LANES), F32), pltpu.VMEM((GDN_HEADS, S), F32)],
        compiler_params=_cparams(("arbitrary", "arbitrary")),
        name="gdn",
    )(proj, proj, proj, proj, proj, conv_w, conv_w, conv_w, pad(a_log), pad(dt_bias),
      out_norm.reshape(1, GDN_HEAD_DIM))


def _s5_disc_kernel(are_ref, aim_ref, ls_ref, bre_ref, bim_ref, lr_ref, li_ref, br_ref, bi_ref):
    lam_re = jnp.minimum(are_ref[...], S5_MAX_RE)
    lam_im = aim_ref[...]
    step = jnp.exp(ls_ref[...])
    mag = jnp.exp(lam_re * step)
    lbr = mag * jnp.cos(lam_im * step)
    lbi = mag * jnp.sin(lam_im * step)
    lr_ref[...] = lbr
    li_ref[...] = lbi
    nr = lbr - 1.0
    den = lam_re * lam_re + lam_im * lam_im
    cr = (nr * lam_re + lbi * lam_im) / den
    ci = (lbi * lam_re - nr * lam_im) / den
    bre = bre_ref[...]
    bim = bim_ref[...]
    br_ref[...] = cr * bre - ci * bim
    bi_ref[...] = cr * bim + ci * bre


def _s5_discretize(a_re, a_im, log_step, b_re, b_im):
    G, P = a_re.shape
    I = b_re.shape[-1]
    gp = jax.ShapeDtypeStruct((G, 1, P), F32)
    gip = jax.ShapeDtypeStruct((G, I, P), F32)
    return pl.pallas_call(
        _s5_disc_kernel, out_shape=(gp, gp, gip, gip), name="s5_discretize",
    )(a_re.reshape(G, 1, P), a_im.reshape(G, 1, P), log_step.reshape(G, 1, 1),
      jnp.swapaxes(b_re, 1, 2), jnp.swapaxes(b_im, 1, 2))


def _s5_kernel(u_ref, bre_ref, bim_ref, lr_ref, li_ref, cre_ref, cim_ref, d_ref, gw_ref, gb_ref,
               o_ref, xr_s, xi_s, cr_s, ci_s, y_s, *, B, tt):
    TS = S5_TILE_STATES

    @pl.when(pl.program_id(0) == 0)
    def _():
        cr_s[...] = jnp.zeros_like(cr_s)
        ci_s[...] = jnp.zeros_like(ci_s)

    NK = TS // LANES
    u = u_ref[...].reshape(B * tt, S5_WIDTH)
    for t in range(S5_TILES):
        ut = u[:, t * LANES:(t + 1) * LANES].astype(BF16)
        br = _dot(ut, bre_ref[t])
        bi = _dot(ut, bim_ref[t])
        for k in range(NK):
            xr_s[t * NK + k] = br[:, k * LANES:(k + 1) * LANES]
            xi_s[t * NK + k] = bi[:, k * LANES:(k + 1) * LANES]

    for t in range(S5_TILES):
        ar = [lr_ref[:, (t * NK + k) * LANES:(t * NK + k + 1) * LANES] for k in range(NK)]
        ai = [li_ref[:, (t * NK + k) * LANES:(t * NK + k + 1) * LANES] for k in range(NK)]

        def step(s, carry, t=t, ar=ar, ai=ai):
            rows = pl.ds(s, B, stride=tt)
            out = []
            for k in range(NK):
                xr, xi = carry[k]
                nr = ar[k] * xr - ai[k] * xi + xr_s[t * NK + k, rows, :]
                ni = ar[k] * xi + ai[k] * xr + xi_s[t * NK + k, rows, :]
                xr_s[t * NK + k, rows, :] = nr
                xi_s[t * NK + k, rows, :] = ni
                out.append((nr, ni))
            return tuple(out)

        init = tuple((cr_s[t * NK + k], ci_s[t * NK + k]) for k in range(NK))
        fin = lax.fori_loop(0, tt, step, init)
        for k in range(NK):
            cr_s[t * NK + k] = fin[k][0]
            ci_s[t * NK + k] = fin[k][1]

    for t in range(S5_TILES):
        lo = slice(t * LANES, (t + 1) * LANES)
        xr = jnp.concatenate([xr_s[t * NK + k] for k in range(NK)], axis=1).astype(BF16)
        xi = jnp.concatenate([xi_s[t * NK + k] for k in range(NK)], axis=1).astype(BF16)
        y = _dot(xr, cre_ref[t]) - _dot(xi, cim_ref[t]) + d_ref[:, lo] * u[:, lo]
        y_s[:, lo] = jax.nn.gelu(y).astype(BF16)
    z = _dot(y_s[...], gw_ref[...]) + gb_ref[...]
    out = z[:, :S5_WIDTH] * jax.nn.sigmoid(z[:, S5_WIDTH:])
    o_ref[...] = out.reshape(B, tt, S5_WIDTH).astype(o_ref.dtype)


def _blockdiag_in(m):
    t = m.reshape(S5_TILES, S5_TILE_GROUPS, S5_GROUP, S5_STATE)
    eye = jnp.eye(S5_TILE_GROUPS, dtype=m.dtype)
    return jnp.einsum('tgip,gh->tgihp', t, eye).reshape(S5_TILES, LANES, S5_TILE_STATES)


def _blockdiag_out(m):
    t = m.reshape(S5_TILES, S5_TILE_GROUPS, S5_GROUP, S5_STATE)
    eye = jnp.eye(S5_TILE_GROUPS, dtype=m.dtype)
    return jnp.einsum('tgip,gh->tgphi', t, eye).reshape(S5_TILES, S5_TILE_STATES, LANES)


def _s5(proj, a_re, a_im, b_re, b_im, c_re, c_im, d_skip, log_step, glu_w, glu_b, tt=32):
    B, S, _ = proj.shape
    lr, li, bbr, bbi = _s5_discretize(a_re, a_im, log_step, b_re, b_im)
    NS = S5_GROUPS * S5_STATE
    kern = functools.partial(_s5_kernel, B=B, tt=tt)
    full2 = lambda i: (0, 0)
    full3 = lambda i: (0, 0, 0)
    return pl.pallas_call(
        kern,
        grid=(S // tt,),
        in_specs=[pl.BlockSpec((B, tt, S5_WIDTH), lambda i: (0, i, OFF_S5 // S5_WIDTH)),
                  pl.BlockSpec((S5_TILES, LANES, S5_TILE_STATES), full3),
                  pl.BlockSpec((S5_TILES, LANES, S5_TILE_STATES), full3),
                  pl.BlockSpec((1, NS), full2),
                  pl.BlockSpec((1, NS), full2),
                  pl.BlockSpec((S5_TILES, S5_TILE_STATES, LANES), full3),
                  pl.BlockSpec((S5_TILES, S5_TILE_STATES, LANES), full3),
                  pl.BlockSpec((1, S5_WIDTH), full2),
                  pl.BlockSpec((S5_WIDTH, 2 * S5_WIDTH), full2),
                  pl.BlockSpec((1, 2 * S5_WIDTH), full2)],
        out_specs=pl.BlockSpec((B, tt, S5_WIDTH), lambda i: (0, i, 0)),
        out_shape=jax.ShapeDtypeStruct((B, S, S5_WIDTH), BF16),
        scratch_shapes=[pltpu.VMEM((NS // LANES, B * tt, LANES), F32),
                        pltpu.VMEM((NS // LANES, B * tt, LANES), F32),
                        pltpu.VMEM((NS // LANES, B, LANES), F32),
                        pltpu.VMEM((NS // LANES, B, LANES), F32),
                        pltpu.VMEM((B * tt, S5_WIDTH), BF16)],
        compiler_params=_cparams(("arbitrary",)),
        name="s5",
    )(proj, _blockdiag_in(bbr).astype(BF16), _blockdiag_in(bbi).astype(BF16),
      lr.reshape(1, NS), li.reshape(1, NS),
      _blockdiag_out(c_re).astype(BF16), _blockdiag_out(c_im).astype(BF16),
      d_skip.reshape(1, S5_WIDTH), glu_w.astype(BF16), glu_b.reshape(1, 2 * S5_WIDTH))


def _dil_kernel(q0, q1, q2, k0, k1, k2, v0, v1, v2, qg_ref, kg_ref, sl_ref, o_ref,
                qs, ks, vs, og, mg, lg, oacc, macc, lacc, *, S):
    q_refs, k_refs, v_refs = (q0, q1, q2), (k0, k1, k2), (v0, v1, v2)
    BLK = DIL_BLK
    HD = DIL_HEAD_DIM
    lane = lax.broadcasted_iota(jnp.int32, (1, LANES), 1)
    first = lane < HD
    ii = lax.broadcasted_iota(jnp.int32, (BLK, BLK), 0)
    jj = lax.broadcasted_iota(jnp.int32, (BLK, BLK), 1)
    steps_cur = (ii - jj).astype(F32)
    steps_prev = (ii - jj + BLK).astype(F32)
    ok_cur = ii >= jj
    ok_prev = jj >= ii

    def rms_pair(x, gain):
        x2 = x * x
        sa = jnp.sum(jnp.where(first, x2, 0.0), axis=-1, keepdims=True)
        sb = jnp.sum(jnp.where(first, 0.0, x2), axis=-1, keepdims=True)
        ms = jnp.where(first, sa, sb) * (1.0 / HD)
        return x * lax.rsqrt(ms + EPS) * gain

    for g, (window, d) in enumerate(DIL_PAIRS):
        sd = S // d
        nb = sd // BLK
        for r in range(d):
            src = pl.ds(r, sd, stride=d) if d > 1 else slice(None)
            dst = slice(r * sd, (r + 1) * sd)
            qs[dst, :] = rms_pair(q_refs[g][0, src, :], qg_ref[...]) * (HD ** -0.5)
            ks[dst, :] = rms_pair(k_refs[g][0, src, :], kg_ref[...])
            vs[dst, :] = v_refs[g][0, src, :]

        sl_a = sl_ref[0, g:g + 1, 0:1] * float(d)
        sl_b = sl_ref[0, g:g + 1, HD:HD + 1] * float(d)
        bias = [(jnp.where(ok_cur, -s * steps_cur, NEG), jnp.where(ok_prev, -s * steps_prev, NEG))
                for s in (sl_a, sl_b)]

        def block(idx, carry, nb=nb, bias=bias):
            st = pl.multiple_of(idx * BLK, BLK)
            rows = pl.ds(st, BLK)
            q = qs[rows, :]
            kc = ks[rows, :]
            vc = vs[rows, :]
            if nb > 1:
                stp = pl.multiple_of(jnp.maximum(st - BLK, 0), BLK)
                kp = ks[pl.ds(stp, BLK), :]
                vp = vs[pl.ds(stp, BLK), :]
                pen = jnp.where((idx % nb) > 0, 0.0, NEG)
            outs = []
            for hh in range(2):
                qh = jnp.where(first, q, 0.0) if hh == 0 else jnp.where(first, 0.0, q)
                sc = _dot_nt(qh, kc) + bias[hh][0]
                m = jnp.max(sc, axis=-1, keepdims=True)
                if nb > 1:
                    spv = _dot_nt(qh, kp) + (bias[hh][1] + pen)
                    m = jnp.maximum(m, jnp.max(spv, axis=-1, keepdims=True))
                pc = jnp.exp(sc - m)
                l = jnp.sum(pc, axis=-1, keepdims=True)
                o = _dot(pc, vc)
                if nb > 1:
                    pp = jnp.exp(spv - m)
                    l = l + jnp.sum(pp, axis=-1, keepdims=True)
                    o = o + _dot(pp, vp)
                outs.append((o, m, l))
            og[rows, :] = jnp.where(first, outs[0][0], outs[1][0])
            mg[rows, :] = jnp.where(first, outs[0][1], outs[1][1])
            lg[rows, :] = jnp.where(first, outs[0][2], outs[1][2])
            return carry

        lax.fori_loop(0, S // BLK, block, 0)

        if g == 0:
            oacc[...] = og[...]
            macc[...] = mg[...]
            lacc[...] = lg[...]
        else:
            for r in range(d):
                tok = pl.ds(r, sd, stride=d)
                res = slice(r * sd, (r + 1) * sd)
                m_old = macc[tok, :]
                m_g = mg[res, :]
                m_new = jnp.maximum(m_old, m_g)
                a = jnp.exp(m_old - m_new)
                b = jnp.exp(m_g - m_new)
                lacc[tok, :] = a * lacc[tok, :] + b * lg[res, :]
                oacc[tok, :] = a * oacc[tok, :] + b * og[res, :]
                macc[tok, :] = m_new

    o_ref[0] = (oacc[...] / lacc[...]).astype(o_ref.dtype)


def _dil(proj, q_norm, k_norm):
    B, S, _ = proj.shape
    sub = jnp.arange(1, DIL_SUBHEADS + 1, dtype=F32)
    slopes = jnp.power(2.0, -ALIBI_MAX * sub / DIL_SUBHEADS)
    sl = slopes.reshape(len(DIL_PAIRS), 2, 2).transpose(1, 0, 2)
    sl = jnp.repeat(sl, DIL_HEAD_DIM, axis=2)
    tile2 = lambda v: jnp.tile(v.reshape(1, DIL_HEAD_DIM), (1, 2))
    kern = functools.partial(_dil_kernel, S=S)
    col = lambda base, g: (lambda b, hp: (b, 0, base // LANES + 2 * g + hp))
    specs = [pl.BlockSpec((1, S, LANES), col(base, g))
             for base in (OFF_CQ, OFF_CK, OFF_CV) for g in range(len(DIL_PAIRS))]
    one = lambda b, hp: (0, 0)
    seq = lambda: pltpu.VMEM((S, LANES), F32)
    return pl.pallas_call(
        kern,
        grid=(B, 2),
        in_specs=specs + [pl.BlockSpec((1, LANES), one), pl.BlockSpec((1, LANES), one),
                          pl.BlockSpec((1, len(DIL_PAIRS), LANES), lambda b, hp: (hp, 0, 0))],
        out_specs=pl.BlockSpec((1, S, LANES), lambda b, hp: (b, 0, hp)),
        out_shape=jax.ShapeDtypeStruct((B, S, DIL_OUT), BF16),
        scratch_shapes=[seq() for _ in range(9)],
        compiler_params=_cparams(("arbitrary", "arbitrary")),
        name="dilated_attn",
    )(*([proj] * 9), tile2(q_norm), tile2(k_norm), sl)


def _merge_kernel(x_ref, ya_ref, yb_ref, yc_ref, g0_ref, g1_ref, g2_ref, gm_ref,
                  wa_ref, wb_ref, wc_ref, wo_ref, o_ref):
    merged = (jax.nn.sigmoid(g0_ref[0]) * _dot(ya_ref[0], wa_ref[...])
              + jax.nn.sigmoid(g1_ref[0]) * _dot(yb_ref[0], wb_ref[...])
              + jax.nn.sigmoid(g2_ref[0]) * _dot(yc_ref[0], wc_ref[...]))
    out = _dot(merged.astype(BF16), wo_ref[...])
    o_ref[0] = x_ref[0] + gm_ref[0] * out


def _merge(x, ya, yb, yc, proj, gmod, wa, wb, wc, wo, tm=256):
    B, S, D = x.shape
    nS = S // tm
    rmap = lambda i: (i // nS, i % nS, 0)
    gate = lambda k: (lambda i: (i // nS, i % nS, OFF_GATES // D + k))
    const = lambda i: (0, 0)
    wspec = lambda w: pl.BlockSpec(w.shape, const, pipeline_mode=pl.Buffered(1))
    return pl.pallas_call(
        _merge_kernel,
        grid=(B * nS,),
        in_specs=[pl.BlockSpec((1, tm, D), rmap),
                  pl.BlockSpec((1, tm, ya.shape[-1]), rmap),
                  pl.BlockSpec((1, tm, yb.shape[-1]), rmap),
                  pl.BlockSpec((1, tm, yc.shape[-1]), rmap),
                  pl.BlockSpec((1, tm, D), gate(0)),
                  pl.BlockSpec((1, tm, D), gate(1)),
                  pl.BlockSpec((1, tm, D), gate(2)),
                  pl.BlockSpec((1, 1, D), lambda i: (i // nS, 0, 0)),
                  wspec(wa), wspec(wb), wspec(wc), wspec(wo)],
        out_specs=pl.BlockSpec((1, tm, D), rmap),
        out_shape=jax.ShapeDtypeStruct((B, S, D), F32),
        compiler_params=_cparams(("arbitrary",)),
        name="merge_out",
    )(x, ya, yb, yc, proj, proj, proj, gmod.reshape(B, 1, D), wa, wb, wc, wo)


def _permute_w_in(w_in, D):
    o = 0
    aq = w_in[:, o:o + 3 * GDN_WIDTH]; o += 3 * GDN_WIDTH
    az = w_in[:, o:o + GDN_WIDTH]; o += GDN_WIDTH
    ba = w_in[:, o:o + 2 * GDN_HEADS]; o += 2 * GDN_HEADS
    bu = w_in[:, o:o + S5_WIDTH]; o += S5_WIDTH
    cq = w_in[:, o:o + 3 * DIL_WIDTH]; o += 3 * DIL_WIDTH
    gt = w_in[:, o:o + N_BRANCH * D]
    pad = jnp.zeros((w_in.shape[0], LANES - 2 * GDN_HEADS), w_in.dtype)
    return jnp.concatenate([bu, cq, az, gt, aq, ba, pad], axis=1).astype(BF16)


def kernel(x, c, ada_w, ada_b, norm_ffn1, ffn1_w1, ffn1_w3, ffn1_w2, norm_mix, w_in, gdn_conv, gdn_a_log, gdn_dt_bias, gdn_out_norm, s5_a_re, s5_a_im, s5_b_re, s5_b_im, s5_c_re, s5_c_im, s5_d, s5_log_step, s5_glu_w, s5_glu_b, dil_q_norm, dil_k_norm, w_branch_a, w_branch_b, w_branch_c, w_out, norm_ffn2, ffn2_w1, ffn2_w3, ffn2_w2):
    B, S, D = x.shape
    L = ada_w.shape[0]
    off_aq = OFF_GATES + N_BRANCH * D
    off_ba = off_aq + 3 * GDN_WIDTH
    mods = _ada_mod(c, ada_w, ada_b)
    for l in range(L):
        sh1, sc1, g1, sh2, sc2, g2, sh3, sc3, g3 = [mods[l, :, i * D:(i + 1) * D] for i in range(N_MOD)]
        x = _ffn(x, norm_ffn1[l], sh1, sc1, g1, ffn1_w1[l].astype(BF16), ffn1_w3[l].astype(BF16),
                 ffn1_w2[l].astype(BF16))
        proj = _proj(x, norm_mix[l], sh2, sc2, _permute_w_in(w_in[l], D))
        ya = _gdn(proj, gdn_conv[l], gdn_a_log[l], gdn_dt_bias[l], gdn_out_norm[l], off_aq, off_ba)
        yb = _s5(proj, s5_a_re[l], s5_a_im[l], s5_b_re[l], s5_b_im[l], s5_c_re[l], s5_c_im[l],
                 s5_d[l], s5_log_step[l], s5_glu_w[l], s5_glu_b[l])
        yc = _dil(proj, dil_q_norm[l], dil_k_norm[l])
        x = _merge(x, ya, yb, yc, proj, g2, w_branch_a[l].astype(BF16), w_branch_b[l].astype(BF16),
                   w_branch_c[l].astype(BF16), w_out[l].astype(BF16))
        x = _ffn(x, norm_ffn2[l], sh3, sc3, g3, ffn2_w1[l].astype(BF16), ffn2_w3[l].astype(BF16),
                 ffn2_w2[l].astype(BF16))
    return x
```

```python
import functools
import math

import jax
import jax.numpy as jnp
from jax import lax
from jax.experimental import pallas as pl
from jax.experimental.pallas import tpu as pltpu

F32 = jnp.float32
BF16 = jnp.bfloat16
EPS = 1e-6
NEG = -1e30

LANES = 128
V7X_VMEM_LIMIT_BYTES = 56 * 1024 * 1024

FFN_RES = 0.5
N_MOD = 9
GDN_HEADS = 8
GDN_HEAD_DIM = 128
GDN_WIDTH = GDN_HEADS * GDN_HEAD_DIM
GDN_CONV = 4
GDN_CHUNK = 64
GDN_PHASE1_CHUNKS = 4
S5_GROUP = 16
S5_WIDTH = 768
S5_GROUPS = S5_WIDTH // S5_GROUP
S5_STATE = 64
S5_MAX_RE = -1e-4
S5_TILE_GROUPS = LANES // S5_GROUP
S5_TILES = S5_GROUPS // S5_TILE_GROUPS
S5_TILE_STATES = S5_TILE_GROUPS * S5_STATE
S5_SCAN_UNROLL = 4
DIL_PAIRS = ((128, 1), (512, 4), (2048, 16))
DIL_HEADS_PER_GROUP = 4
DIL_HEAD_DIM = 64
DIL_SUBHEADS = len(DIL_PAIRS) * DIL_HEADS_PER_GROUP
DIL_WIDTH = DIL_SUBHEADS * DIL_HEAD_DIM
DIL_OUT = DIL_HEADS_PER_GROUP * DIL_HEAD_DIM
DIL_BLK = 128
ALIBI_MAX = 8.0
N_BRANCH = 3

OFF_S5 = 0
OFF_CQ = OFF_S5 + S5_WIDTH
OFF_CK = OFF_CQ + DIL_WIDTH
OFF_CV = OFF_CK + DIL_WIDTH
OFF_Z = OFF_CV + DIL_WIDTH
OFF_GATES = OFF_Z + GDN_WIDTH
OFF_AQ = None


def _cparams(sem, vmem=V7X_VMEM_LIMIT_BYTES):
    return pltpu.CompilerParams(dimension_semantics=sem, vmem_limit_bytes=vmem)


def _dot(a, b, precision=None):
    return jnp.dot(a, b, preferred_element_type=F32, precision=precision)


def _dot_nt(a, b, precision=None):
    return lax.dot_general(a, b, (((1,), (1,)), ((), ())), preferred_element_type=F32,
                           precision=precision)


def _dot_tn(a, b, precision=None):
    return lax.dot_general(a, b, (((0,), (0,)), ((), ())), preferred_element_type=F32,
                           precision=precision)


def _silu(x):
    return x * jax.nn.sigmoid(x)


def _ada_kernel(c_ref, w_ref, b_ref, o_ref):
    c = c_ref[...]
    o_ref[0] = _dot(_silu(c), w_ref[0]) + b_ref[0]


def _ada_mod(c, ada_w, ada_b, tn=1024):
    L, D, N = ada_w.shape
    B = c.shape[0]
    return pl.pallas_call(
        _ada_kernel,
        grid=(L, N // tn),
        in_specs=[pl.BlockSpec((B, D), lambda l, j: (0, 0)),
                  pl.BlockSpec((1, D, tn), lambda l, j: (l, 0, j)),
                  pl.BlockSpec((1, 1, tn), lambda l, j: (l, 0, j))],
        out_specs=pl.BlockSpec((1, B, tn), lambda l, j: (l, 0, j)),
        out_shape=jax.ShapeDtypeStruct((L, B, N), F32),
        compiler_params=_cparams(("arbitrary", "arbitrary")),
        name="ada_mod",
    )(c, ada_w, ada_b.reshape(L, 1, N))


def _norm_mod(x, g, sh, sc):
    ms = jnp.mean(x * x, axis=-1, keepdims=True)
    y = x * lax.rsqrt(ms + EPS) * g
    return y * (1.0 + sc) + sh


def _ffn_kernel(x_ref, g_ref, sh_ref, sc_ref, gate_ref, w1_ref, w3_ref, w2_ref, o_ref,
                hn_ref, a_ref, *, nF, tn):
    j = pl.program_id(1)

    @pl.when(j == 0)
    def _():
        hn_ref[...] = _norm_mod(x_ref[0], g_ref[...], sh_ref[0], sc_ref[0]).astype(BF16)

    @pl.when(j < nF)
    def _():
        hn = hn_ref[...]
        h1 = _dot(hn, w1_ref[...])
        h3 = _dot(hn, w3_ref[...])
        a_ref[j] = (_silu(h1) * h3).astype(BF16)

    @pl.when(j >= nF)
    def _():
        cols = pl.ds(pl.multiple_of((j - nF) * tn, tn), tn)
        a = jnp.concatenate([a_ref[f] for f in range(nF)], axis=1)
        o_ref[0] = x_ref[0, :, cols] + FFN_RES * gate_ref[0, :, cols] * _dot(a, w2_ref[...])


def _ffn(x, g, sh, sc, gate, w1, w3, w2, tm=512, tf=512, tn=512):
    B, S, D = x.shape
    F = w1.shape[1]
    nS = S // tm
    nF = F // tf
    xmap = lambda i, j: (i // nS, i % nS, 0)
    bmap = lambda i, j: (i // nS, 0, 0)
    up = lambda i, j: (0, jnp.minimum(j, nF - 1))
    down = lambda i, j: (0, jnp.maximum(j - nF, 0))
    return pl.pallas_call(
        functools.partial(_ffn_kernel, nF=nF, tn=tn),
        grid=(B * nS, nF + D // tn),
        in_specs=[pl.BlockSpec((1, tm, D), xmap),
                  pl.BlockSpec((1, D), lambda i, j: (0, 0)),
                  pl.BlockSpec((1, 1, D), bmap),
                  pl.BlockSpec((1, 1, D), bmap),
                  pl.BlockSpec((1, 1, D), bmap),
                  pl.BlockSpec((D, tf), up),
                  pl.BlockSpec((D, tf), up),
                  pl.BlockSpec((F, tn), down)],
        out_specs=pl.BlockSpec((1, tm, tn), lambda i, j: (i // nS, i % nS, jnp.maximum(j - nF, 0))),
        out_shape=jax.ShapeDtypeStruct((B, S, D), F32),
        scratch_shapes=[pltpu.VMEM((tm, D), BF16), pltpu.VMEM((nF, tm, tf), BF16)],
        compiler_params=_cparams(("arbitrary", "arbitrary")),
        name="ffn",
    )(x, g.reshape(1, D), sh.reshape(B, 1, D), sc.reshape(B, 1, D), gate.reshape(B, 1, D),
      w1, w3, w2)


def _proj_kernel(x_ref, g_ref, sh_ref, sc_ref, w_ref, o_ref, hn_ref):
    @pl.when(pl.program_id(1) == 0)
    def _():
        hn_ref[...] = _norm_mod(x_ref[0], g_ref[...], sh_ref[0], sc_ref[0]).astype(BF16)

    o_ref[0] = _dot(hn_ref[...], w_ref[...])


def _proj(x, g, sh, sc, w, tm=1024, tn=1920):
    B, S, D = x.shape
    N = w.shape[1]
    nS = S // tm
    xmap = lambda i, j: (i // nS, i % nS, 0)
    bmap = lambda i, j: (i // nS, 0, 0)
    return pl.pallas_call(
        _proj_kernel,
        grid=(B * nS, N // tn),
        in_specs=[pl.BlockSpec((1, tm, D), xmap, pipeline_mode=pl.Buffered(1)),
                  pl.BlockSpec((1, D), lambda i, j: (0, 0)),
                  pl.BlockSpec((1, 1, D), bmap),
                  pl.BlockSpec((1, 1, D), bmap),
                  pl.BlockSpec((D, tn), lambda i, j: (0, j))],
        out_specs=pl.BlockSpec((1, tm, tn), lambda i, j: (i // nS, i % nS, j)),
        out_shape=jax.ShapeDtypeStruct((B, S, N), F32),
        scratch_shapes=[pltpu.VMEM((tm, D), BF16)],
        compiler_params=_cparams(("arbitrary", "arbitrary")),
        name="in_proj",
    )(x, g.reshape(1, D), sh.reshape(B, 1, D), sc.reshape(B, 1, D), w)


def _gdn_kernel(q_ref, k_ref, v_ref, z_ref, ba_ref, cq_ref, ck_ref, cv_ref, al_ref, dtb_ref,
                on_ref, o_ref, qn_s, kn_s, vv_s, qo_s, ku_s, cols_s, gct_s,
                *, S, HG):
    C = GDN_CHUNK
    HD = GDN_HEAD_DIM
    hg = pl.program_id(1)
    NH = GDN_HEADS
    shift = (LANES - hg * HG) % LANES
    t16 = pltpu.roll(ba_ref[0], shift, 1).T[0:2 * NH, :]
    a_col = al_ref[0][:, 0:1]
    d_col = dtb_ref[0][:, 0:1]
    beta16 = jax.nn.sigmoid(t16)
    xg = t16 + d_col
    g16 = -jnp.exp(a_col) * (jnp.maximum(xg, 0.0) + jnp.log1p(jnp.exp(-jnp.abs(xg))))
    lanec = lax.broadcasted_iota(jnp.int32, (2 * NH, S), 1) % C
    for d in (1, 2, 4, 8, 16, 32):
        g16 = g16 + jnp.where(lanec >= d, pltpu.roll(g16, d, 1), 0.0)
    gl16 = jnp.where(lanec == C - 1, g16, 0.0)
    for d in (1, 2, 4, 8, 16, 32):
        gl16 = gl16 + jnp.where(lanec + d < C, pltpu.roll(gl16, S - d, 1), 0.0)
    gct_s[...] = g16[NH:2 * NH, :]
    cols_s[...] = jnp.concatenate(
        [beta16[0:NH], g16[NH:2 * NH], gl16[NH:2 * NH], jnp.zeros((LANES - 3 * NH, S), F32)], axis=0).T

    SUB = 8
    row8 = lax.broadcasted_iota(jnp.int32, (SUB, HD), 0)

    def conv_silu(x_ref, w_ref, sl):
        w = w_ref[:, sl]
        K = GDN_CONV
        x0 = x_ref[0, 0:SUB, sl]
        head = x0 * w[K - 1:K, :]
        body = x_ref[0, SUB:S, sl] * w[K - 1:K, :]
        for kk in range(K - 1):
            s = K - 1 - kk
            head = head + jnp.where(row8 >= s, pltpu.roll(x0, s, 0), 0.0) * w[kk:kk + 1, :]
            body = body + x_ref[0, SUB - s:S - s, sl] * w[kk:kk + 1, :]
        return _silu(jnp.concatenate([head, body], axis=0))

    def l2n(x):
        return x * lax.rsqrt(jnp.sum(x * x, axis=-1, keepdims=True) + EPS)

    for j in range(HG):
        sl = slice(j * HD, (j + 1) * HD)
        qn_s[:, sl] = l2n(conv_silu(q_ref, cq_ref, sl)) * (HD ** -0.5)
        kn_s[:, sl] = l2n(conv_silu(k_ref, ck_ref, sl))
        vv_s[:, sl] = conv_silu(v_ref, cv_ref, sl)

    ii = lax.broadcasted_iota(jnp.int32, (C, C), 0)
    jj = lax.broadcasted_iota(jnp.int32, (C, C), 1)
    tril = ii >= jj
    strict = ii > jj
    eye = jnp.where(ii == jj, 1.0, 0.0)
    bd = {w: (ii // w) == (jj // w) for w in (8, 16, 32)}
    base_mask = strict & bd[8]
    merge_masks = (bd[16] & ~bd[8], bd[32] & ~bd[16], ~bd[32])
    NC = GDN_PHASE1_CHUNKS

    def phase1(p, carry):
        r0 = pl.multiple_of(p * NC * C, NC * C)
        gct = gct_s[:, pl.ds(r0, NC * C)]
        chains = []
        for cc in range(NC):
            rows = pl.ds(r0 + cc * C, C)
            cols = cols_s[rows, :]
            for j in range(HG):
                sl = slice(j * HD, (j + 1) * HD)
                gcol = cols[:, NH + j:NH + j + 1]
                glcol = cols[:, 2 * NH + j:2 * NH + j + 1]
                grow = gct[j:j + 1, cc * C:(cc + 1) * C]
                bcol = cols[:, j:j + 1]
                kn_c = kn_s[rows, sl]
                kb = kn_c * bcol
                eg = jnp.exp(gcol)
                chains.append(dict(
                    cc=cc, j=j, rows=rows, kn=kn_c, kb=kb, qn=qn_s[rows, sl], eg=eg,
                    decay=jnp.where(tril, jnp.exp(jnp.minimum(gcol - grow, 0.0)), 0.0),
                    rhs=jnp.concatenate([vv_s[rows, sl] * bcol, kb * eg], axis=1),
                    kdec=kn_c * jnp.exp(glcol - gcol)))
        for ch in chains:
            ch['lm'] = jnp.where(strict, _dot_nt(ch['kb'], ch['kn']) * ch['decay'], 0.0)
        for ch in chains:
            ch['attn'] = _dot_nt(ch['qn'], ch['kn']) * ch['decay']
        for ch in chains:
            ch['l8'] = jnp.where(base_mask, ch['lm'], 0.0)
            ch['s2'] = _dot(ch['l8'], ch['l8'])
        for ch in chains:
            n1 = eye - ch['l8']
            ch['n2'] = n1 + _dot(n1, ch['s2'])
        for ch in chains:
            ch['s4'] = _dot(ch['s2'], ch['s2'])
        for ch in chains:
            ch['t'] = ch['n2'] + _dot(ch['n2'], ch['s4'])
        for mm in merge_masks:
            for ch in chains:
                ch['tl'] = _dot(ch['t'], jnp.where(mm, ch['lm'], 0.0))
            for ch in chains:
                ch['t'] = ch['t'] - _dot(ch['tl'], ch['t'])
        for ch in chains:
            ch['x'] = _dot(ch['t'], ch['rhs'])
        for ch in chains:
            au = _dot(ch['attn'], ch['x'])
            j = ch['j']
            qo_s[ch['rows'], j * 2 * HD:(j + 1) * 2 * HD] = jnp.concatenate(
                [au[:, :HD], ch['qn'] * ch['eg'] - au[:, HD:]], axis=1)
        for ch in chains:
            ku_s[p * NC + ch['cc'], ch['j']] = _dot_tn(ch['kdec'], ch['x'])
        return carry

    lax.fori_loop(0, S // (NC * C), phase1, 0)

    def phase2(c, hs):
        rows = pl.ds(pl.multiple_of(c * C, C), C)
        glc = cols_s[rows, :]
        new_hs = []
        for j in range(HG):
            sl = slice(j * HD, (j + 1) * HD)
            h = hs[j]
            ku = ku_s[c, j]
            qo = qo_s[rows, j * 2 * HD:(j + 1) * 2 * HD]
            r = _dot(jnp.concatenate([ku[:, HD:], qo[:, HD:]], axis=0), h)
            o = r[HD:] + qo[:, :HD]
            egl = jnp.exp(glc[0:1, 2 * NH + j:2 * NH + j + 1])
            new_hs.append(h * egl - r[:HD] + ku[:, :HD])
            on = o * lax.rsqrt(jnp.mean(o * o, axis=-1, keepdims=True) + EPS) * on_ref[...]
            zc = z_ref[0, rows, sl]
            o_ref[0, rows, sl] = (on * _silu(zc)).astype(o_ref.dtype)
        return tuple(new_hs)

    h0 = tuple(jnp.zeros((HD, HD), F32) for _ in range(HG))
    lax.fori_loop(0, S // C, phase2, h0)


def _gdn(proj, conv_w, a_log, dt_bias, out_norm, off_aq, off_ba, HG=2):
    B, S, _ = proj.shape
    W = HG * GDN_HEAD_DIM
    nG = GDN_HEADS // HG

    def pad(v):
        rows = jnp.zeros((nG, 2 * GDN_HEADS), F32).at[:, GDN_HEADS:GDN_HEADS + HG].set(v.reshape(nG, HG))
        return jnp.broadcast_to(rows[:, :, None], (nG, 2 * GDN_HEADS, LANES))

    kern = functools.partial(_gdn_kernel, S=S, HG=HG)
    par = pl.BlockSpec((1, 2 * GDN_HEADS, LANES), lambda b, h: (h, 0, 0))
    col = lambda base: (lambda b, h: (b, 0, base // W + h))
    cw = lambda base: (lambda b, h: (0, base // W + h))
    one = lambda b, h: (0, 0)
    return pl.pallas_call(
        kern,
        grid=(B, nG),
        in_specs=[pl.BlockSpec((1, S, W), col(off_aq)),
                  pl.BlockSpec((1, S, W), col(off_aq + GDN_WIDTH)),
                  pl.BlockSpec((1, S, W), col(off_aq + 2 * GDN_WIDTH)),
                  pl.BlockSpec((1, S, W), col(OFF_Z)),
                  pl.BlockSpec((1, S, LANES), lambda b, h: (b, 0, off_ba // LANES)),
                  pl.BlockSpec((GDN_CONV, W), cw(0)),
                  pl.BlockSpec((GDN_CONV, W), cw(GDN_WIDTH)),
                  pl.BlockSpec((GDN_CONV, W), cw(2 * GDN_WIDTH)),
                  par, par,
                  pl.BlockSpec((1, GDN_HEAD_DIM), one)],
        out_specs=pl.BlockSpec((1, S, W), lambda b, h: (b, 0, h)),
        out_shape=jax.ShapeDtypeStruct((B, S, GDN_WIDTH), BF16),
        scratch_shapes=[pltpu.VMEM((S, W), F32), pltpu.VMEM((S, W), F32), pltpu.VMEM((S, W), F32),
                        pltpu.VMEM((S, 2 * W), F32),
                        pltpu.VMEM((S // GDN_CHUNK, HG, GDN_HEAD_DIM, 2 * GDN_HEAD_DIM), F32),
                        pltpu.VMEM((S, LANES), F32), pltpu.VMEM((GDN_HEADS, S), F32)],
        compiler_params=_cparams(("arbitrary", "arbitrary")),
        name="gdn",
    )(proj, proj, proj, proj, proj, conv_w, conv_w, conv_w, pad(a_log), pad(dt_bias),
      out_norm.reshape(1, GDN_HEAD_DIM))


def _s5_disc_kernel(are_ref, aim_ref, ls_ref, bre_ref, bim_ref, lr_ref, li_ref, br_ref, bi_ref):
    lam_re = jnp.minimum(are_ref[...], S5_MAX_RE)
    lam_im = aim_ref[...]
    step = jnp.exp(ls_ref[...])
    mag = jnp.exp(lam_re * step)
    lbr = mag * jnp.cos(lam_im * step)
    lbi = mag * jnp.sin(lam_im * step)
    lr_ref[...] = lbr
    li_ref[...] = lbi
    nr = lbr - 1.0
    den = lam_re * lam_re + lam_im * lam_im
    cr = (nr * lam_re + lbi * lam_im) / den
    ci = (lbi * lam_re - nr * lam_im) / den
    bre = bre_ref[...]
    bim = bim_ref[...]
    br_ref[...] = cr * bre - ci * bim
    bi_ref[...] = cr * bim + ci * bre


def _s5_discretize(a_re, a_im, log_step, b_re, b_im):
    G, P = a_re.shape
    I = b_re.shape[-1]
    gp = jax.ShapeDtypeStruct((G, 1, P), F32)
    gip = jax.ShapeDtypeStruct((G, I, P), F32)
    return pl.pallas_call(
        _s5_disc_kernel, out_shape=(gp, gp, gip, gip), name="s5_discretize",
    )(a_re.reshape(G, 1, P), a_im.reshape(G, 1, P), log_step.reshape(G, 1, 1),
      jnp.swapaxes(b_re, 1, 2), jnp.swapaxes(b_im, 1, 2))


def _s5_kernel(u_ref, pf_ref, pb_ref, bb_ref, lr_ref, li_ref, cc_ref, d_ref, gw_ref, gb_ref,
               o_ref, xr_s, xi_s, cr_s, ci_s, y_s, *, B, tt):
    TS = S5_TILE_STATES
    R = B * tt

    @pl.when(pl.program_id(0) == 0)
    def _():
        cr_s[...] = jnp.zeros_like(cr_s)
        ci_s[...] = jnp.zeros_like(ci_s)

    u = u_ref[...].reshape(R, S5_WIDTH)
    u_hi = u.astype(BF16)
    u_lo = (u - u_hi.astype(F32)).astype(BF16)
    up = _dot(pf_ref[...], jnp.concatenate([u_hi, u_lo], axis=1))
    u_tb = up[:, :S5_WIDTH] + up[:, S5_WIDTH:]
    for t in range(S5_TILES):
        ut = up[:, t * LANES:(t + 1) * LANES].astype(BF16)
        bu = _dot(ut, bb_ref[t])
        xr_s[:, :, t * TS:(t + 1) * TS] = bu[:, :TS].reshape(tt, B, TS)
        xi_s[:, :, t * TS:(t + 1) * TS] = bu[:, TS:].reshape(tt, B, TS)

    for t in range(S5_TILES):
        ls = slice(t * TS, (t + 1) * TS)
        ar = lr_ref[:, ls]
        ai = li_ref[:, ls]

        def step(s, carry, ls=ls, ar=ar, ai=ai):
            xr, xi = carry
            nr = ar * xr - ai * xi + xr_s[s, :, ls]
            ni = ar * xi + ai * xr + xi_s[s, :, ls]
            xr_s[s, :, ls] = nr
            xi_s[s, :, ls] = ni
            return nr, ni

        xr, xi = lax.fori_loop(0, tt, step, (cr_s[:, ls], ci_s[:, ls]), unroll=S5_SCAN_UNROLL)
        cr_s[:, ls] = xr
        ci_s[:, ls] = xi

    for t in range(S5_TILES):
        ls = slice(t * TS, (t + 1) * TS)
        lo = slice(t * LANES, (t + 1) * LANES)
        x = jnp.concatenate([xr_s[:, :, ls].reshape(R, TS), xi_s[:, :, ls].reshape(R, TS)], axis=1)
        y = _dot(x.astype(BF16), cc_ref[t]) + d_ref[:, lo] * u_tb[:, lo]
        y_s[:, lo] = jax.nn.gelu(y).astype(BF16)
    z = _dot(y_s[...], gw_ref[...]) + gb_ref[...]
    out = (z[:, :S5_WIDTH] * jax.nn.sigmoid(z[:, S5_WIDTH:])).astype(BF16)
    o_ref[...] = _dot(pb_ref[...], out).reshape(B, tt, S5_WIDTH).astype(o_ref.dtype)


def _blockdiag_in(m):
    t = m.reshape(S5_TILES, S5_TILE_GROUPS, S5_GROUP, S5_STATE)
    eye = jnp.eye(S5_TILE_GROUPS, dtype=m.dtype)
    return jnp.einsum('tgip,gh->tgihp', t, eye).reshape(S5_TILES, LANES, S5_TILE_STATES)


def _blockdiag_out(m):
    t = m.reshape(S5_TILES, S5_TILE_GROUPS, S5_GROUP, S5_STATE)
    eye = jnp.eye(S5_TILE_GROUPS, dtype=m.dtype)
    return jnp.einsum('tgip,gh->tgphi', t, eye).reshape(S5_TILES, S5_TILE_STATES, LANES)


def _s5(proj, a_re, a_im, b_re, b_im, c_re, c_im, d_skip, log_step, glu_w, glu_b, tt=32):
    B, S, _ = proj.shape
    lr, li, bbr, bbi = _s5_discretize(a_re, a_im, log_step, b_re, b_im)
    NS = S5_GROUPS * S5_STATE
    kern = functools.partial(_s5_kernel, B=B, tt=tt)
    R = B * tt
    src = (jnp.arange(R) % B) * tt + jnp.arange(R) // B
    pf = (src[:, None] == jnp.arange(R)[None, :]).astype(BF16)
    bb = jnp.concatenate([_blockdiag_in(bbr), _blockdiag_in(bbi)], axis=2).astype(BF16)
    cc = jnp.concatenate([_blockdiag_out(c_re), -_blockdiag_out(c_im)], axis=1).astype(BF16)
    full2 = lambda i: (0, 0)
    full3 = lambda i: (0, 0, 0)
    return pl.pallas_call(
        kern,
        grid=(S // tt,),
        in_specs=[pl.BlockSpec((B, tt, S5_WIDTH), lambda i: (0, i, OFF_S5 // S5_WIDTH)),
                  pl.BlockSpec((R, R), full2),
                  pl.BlockSpec((R, R), full2),
                  pl.BlockSpec((S5_TILES, LANES, 2 * S5_TILE_STATES), full3),
                  pl.BlockSpec((1, NS), full2),
                  pl.BlockSpec((1, NS), full2),
                  pl.BlockSpec((S5_TILES, 2 * S5_TILE_STATES, LANES), full3),
                  pl.BlockSpec((1, S5_WIDTH), full2),
                  pl.BlockSpec((S5_WIDTH, 2 * S5_WIDTH), full2),
                  pl.BlockSpec((1, 2 * S5_WIDTH), full2)],
        out_specs=pl.BlockSpec((B, tt, S5_WIDTH), lambda i: (0, i, 0)),
        out_shape=jax.ShapeDtypeStruct((B, S, S5_WIDTH), BF16),
        scratch_shapes=[pltpu.VMEM((tt, B, NS), F32), pltpu.VMEM((tt, B, NS), F32),
                        pltpu.VMEM((B, NS), F32), pltpu.VMEM((B, NS), F32),
                        pltpu.VMEM((R, S5_WIDTH), BF16)],
        compiler_params=_cparams(("arbitrary",)),
        name="s5",
    )(proj, pf, pf.T, bb, lr.reshape(1, NS), li.reshape(1, NS), cc,
      d_skip.reshape(1, S5_WIDTH), glu_w.astype(BF16), glu_b.reshape(1, 2 * S5_WIDTH))


def _dil_kernel(q0, q1, q2, k0, k1, k2, v0, v1, v2, qg_ref, kg_ref, sl_ref, o_ref,
                qs, ks, vs, og, mg, lg, oacc, macc, lacc, *, S):
    q_refs, k_refs, v_refs = (q0, q1, q2), (k0, k1, k2), (v0, v1, v2)
    BLK = DIL_BLK
    HD = DIL_HEAD_DIM
    lane = lax.broadcasted_iota(jnp.int32, (1, LANES), 1)
    first = lane < HD
    ii = lax.broadcasted_iota(jnp.int32, (BLK, BLK), 0)
    jj = lax.broadcasted_iota(jnp.int32, (BLK, BLK), 1)
    steps_cur = (ii - jj).astype(F32)
    steps_prev = (ii - jj + BLK).astype(F32)
    ok_cur = ii >= jj
    ok_prev = jj >= ii

    li = lax.broadcasted_iota(jnp.int32, (LANES, LANES), 0) // HD
    lj = lax.broadcasted_iota(jnp.int32, (LANES, LANES), 1) // HD
    avg = jnp.where(li == lj, 1.0 / HD, 0.0).astype(BF16)

    def rms_pair(x, gain):
        x2 = x * x
        hi = x2.astype(BF16)
        lo = (x2 - hi.astype(F32)).astype(BF16)
        ms = _dot(hi, avg) + _dot(lo, avg)
        return x * lax.rsqrt(ms + EPS) * gain

    for g, (window, d) in enumerate(DIL_PAIRS):
        sd = S // d
        nb = sd // BLK
        for r in range(d):
            src = pl.ds(r, sd, stride=d) if d > 1 else slice(None)
            dst = slice(r * sd, (r + 1) * sd)
            qs[dst, :] = rms_pair(q_refs[g][0, src, :], qg_ref[...]) * (HD ** -0.5)
            ks[dst, :] = rms_pair(k_refs[g][0, src, :], kg_ref[...])
            vs[dst, :] = v_refs[g][0, src, :]

        sl_a = sl_ref[0, g:g + 1, 0:1] * float(d)
        sl_b = sl_ref[0, g:g + 1, HD:HD + 1] * float(d)
        bias = [(jnp.where(ok_cur, -s * steps_cur, NEG), jnp.where(ok_prev, -s * steps_prev, NEG))
                for s in (sl_a, sl_b)]

        def block_pair(ip, carry, nb=nb, bias=bias):
            st0 = pl.multiple_of(ip * 2 * BLK, 2 * BLK)
            ch = []
            for bi in range(2):
                rows = pl.ds(st0 + bi * BLK, BLK)
                blk = dict(rows=rows, q=qs[rows, :], kc=ks[rows, :], vc=vs[rows, :])
                if nb > 1 and bi == 0:
                    stp = pl.multiple_of(jnp.maximum(st0 - BLK, 0), BLK)
                    blk.update(kp=ks[pl.ds(stp, BLK), :], vp=vs[pl.ds(stp, BLK), :],
                               pen=jnp.where(((2 * ip) % nb) > 0, 0.0, NEG))
                elif nb > 1:
                    blk.update(kp=ch[0]['blk']['kc'], vp=ch[0]['blk']['vc'], pen=0.0)
                for hh in range(2):
                    qh = jnp.where(first, blk['q'], 0.0) if hh == 0 else jnp.where(first, 0.0, blk['q'])
                    ch.append(dict(blk=blk, hh=hh, qh=qh))
            for c in ch:
                c['sc'] = _dot_nt(c['qh'], c['blk']['kc']) + bias[c['hh']][0]
            if nb > 1:
                for c in ch:
                    c['sp'] = _dot_nt(c['qh'], c['blk']['kp']) + (bias[c['hh']][1] + c['blk']['pen'])
            for c in ch:
                m = jnp.max(c['sc'], axis=-1, keepdims=True)
                if nb > 1:
                    m = jnp.maximum(m, jnp.max(c['sp'], axis=-1, keepdims=True))
                c['m'] = m
                c['pc'] = jnp.exp(c['sc'] - m)
                c['l'] = jnp.sum(c['pc'], axis=-1, keepdims=True)
                if nb > 1:
                    c['pp'] = jnp.exp(c['sp'] - m)
                    c['l'] = c['l'] + jnp.sum(c['pp'], axis=-1, keepdims=True)
            for c in ch:
                c['o'] = _dot(c['pc'], c['blk']['vc'])
            if nb > 1:
                for c in ch:
                    c['o'] = c['o'] + _dot(c['pp'], c['blk']['vp'])
            for bi in range(2):
                a, b = ch[2 * bi], ch[2 * bi + 1]
                rows = a['blk']['rows']
                og[rows, :] = jnp.where(first, a['o'], b['o'])
                mg[rows, :] = jnp.where(first, a['m'], b['m'])
                lg[rows, :] = jnp.where(first, a['l'], b['l'])
            return carry

        lax.fori_loop(0, S // (2 * BLK), block_pair, 0)

        if g == 0:
            oacc[...] = og[...]
            macc[...] = mg[...]
            lacc[...] = lg[...]
        else:
            for r in range(d):
                tok = pl.ds(r, sd, stride=d)
                res = slice(r * sd, (r + 1) * sd)
                m_old = macc[tok, :]
                m_g = mg[res, :]
                m_new = jnp.maximum(m_old, m_g)
                a = jnp.exp(m_old - m_new)
                b = jnp.exp(m_g - m_new)
                lacc[tok, :] = a * lacc[tok, :] + b * lg[res, :]
                oacc[tok, :] = a * oacc[tok, :] + b * og[res, :]
                macc[tok, :] = m_new

    o_ref[0] = (oacc[...] / lacc[...]).astype(o_ref.dtype)


def _dil(proj, q_norm, k_norm):
    B, S, _ = proj.shape
    sub = jnp.arange(1, DIL_SUBHEADS + 1, dtype=F32)
    slopes = jnp.power(2.0, -ALIBI_MAX * sub / DIL_SUBHEADS)
    sl = slopes.reshape(len(DIL_PAIRS), 2, 2).transpose(1, 0, 2)
    sl = jnp.repeat(sl, DIL_HEAD_DIM, axis=2)
    tile2 = lambda v: jnp.tile(v.reshape(1, DIL_HEAD_DIM), (1, 2))
    kern = functools.partial(_dil_kernel, S=S)
    col = lambda base, g: (lambda b, hp: (b, 0, base // LANES + 2 * g + hp))
    specs = [pl.BlockSpec((1, S, LANES), col(base, g))
             for base in (OFF_CQ, OFF_CK, OFF_CV) for g in range(len(DIL_PAIRS))]
    one = lambda b, hp: (0, 0)
    seq = lambda: pltpu.VMEM((S, LANES), F32)
    return pl.pallas_call(
        kern,
        grid=(B, 2),
        in_specs=specs + [pl.BlockSpec((1, LANES), one), pl.BlockSpec((1, LANES), one),
                          pl.BlockSpec((1, len(DIL_PAIRS), LANES), lambda b, hp: (hp, 0, 0))],
        out_specs=pl.BlockSpec((1, S, LANES), lambda b, hp: (b, 0, hp)),
        out_shape=jax.ShapeDtypeStruct((B, S, DIL_OUT), BF16),
        scratch_shapes=[seq() for _ in range(9)],
        compiler_params=_cparams(("arbitrary", "arbitrary")),
        name="dilated_attn",
    )(*([proj] * 9), tile2(q_norm), tile2(k_norm), sl)


def _merge_kernel(x_ref, ya_ref, yb_ref, yc_ref, g0_ref, g1_ref, g2_ref, gm_ref,
                  wa_ref, wb_ref, wc_ref, wo_ref, o_ref):
    merged = (jax.nn.sigmoid(g0_ref[0]) * _dot(ya_ref[0], wa_ref[...])
              + jax.nn.sigmoid(g1_ref[0]) * _dot(yb_ref[0], wb_ref[...])
              + jax.nn.sigmoid(g2_ref[0]) * _dot(yc_ref[0], wc_ref[...]))
    out = _dot(merged.astype(BF16), wo_ref[...])
    o_ref[0] = x_ref[0] + gm_ref[0] * out


def _merge(x, ya, yb, yc, proj, gmod, wa, wb, wc, wo, tm=256):
    B, S, D = x.shape
    nS = S // tm
    rmap = lambda i: (i // nS, i % nS, 0)
    gate = lambda k: (lambda i: (i // nS, i % nS, OFF_GATES // D + k))
    const = lambda i: (0, 0)
    wspec = lambda w: pl.BlockSpec(w.shape, const, pipeline_mode=pl.Buffered(1))
    return pl.pallas_call(
        _merge_kernel,
        grid=(B * nS,),
        in_specs=[pl.BlockSpec((1, tm, D), rmap),
                  pl.BlockSpec((1, tm, ya.shape[-1]), rmap),
                  pl.BlockSpec((1, tm, yb.shape[-1]), rmap),
                  pl.BlockSpec((1, tm, yc.shape[-1]), rmap),
                  pl.BlockSpec((1, tm, D), gate(0)),
                  pl.BlockSpec((1, tm, D), gate(1)),
                  pl.BlockSpec((1, tm, D), gate(2)),
                  pl.BlockSpec((1, 1, D), lambda i: (i // nS, 0, 0)),
                  wspec(wa), wspec(wb), wspec(wc), wspec(wo)],
        out_specs=pl.BlockSpec((1, tm, D), rmap),
        out_shape=jax.ShapeDtypeStruct((B, S, D), F32),
        compiler_params=_cparams(("arbitrary",)),
        name="merge_out",
    )(x, ya, yb, yc, proj, proj, proj, gmod.reshape(B, 1, D), wa, wb, wc, wo)


def _permute_w_in(w_in, D):
    o = 0
    aq = w_in[:, o:o + 3 * GDN_WIDTH]; o += 3 * GDN_WIDTH
    az = w_in[:, o:o + GDN_WIDTH]; o += GDN_WIDTH
    ba = w_in[:, o:o + 2 * GDN_HEADS]; o += 2 * GDN_HEADS
    bu = w_in[:, o:o + S5_WIDTH]; o += S5_WIDTH
    cq = w_in[:, o:o + 3 * DIL_WIDTH]; o += 3 * DIL_WIDTH
    gt = w_in[:, o:o + N_BRANCH * D]
    pad = jnp.zeros((w_in.shape[0], LANES - 2 * GDN_HEADS), w_in.dtype)
    return jnp.concatenate([bu, cq, az, gt, aq, ba, pad], axis=1).astype(BF16)


def kernel(x, c, ada_w, ada_b, norm_ffn1, ffn1_w1, ffn1_w3, ffn1_w2, norm_mix, w_in, gdn_conv, gdn_a_log, gdn_dt_bias, gdn_out_norm, s5_a_re, s5_a_im, s5_b_re, s5_b_im, s5_c_re, s5_c_im, s5_d, s5_log_step, s5_glu_w, s5_glu_b, dil_q_norm, dil_k_norm, w_branch_a, w_branch_b, w_branch_c, w_out, norm_ffn2, ffn2_w1, ffn2_w3, ffn2_w2):
    B, S, D = x.shape
    L = ada_w.shape[0]
    off_aq = OFF_GATES + N_BRANCH * D
    off_ba = off_aq + 3 * GDN_WIDTH
    mods = _ada_mod(c, ada_w, ada_b)
    for l in range(L):
        sh1, sc1, g1, sh2, sc2, g2, sh3, sc3, g3 = [mods[l, :, i * D:(i + 1) * D] for i in range(N_MOD)]
        x = _ffn(x, norm_ffn1[l], sh1, sc1, g1, ffn1_w1[l].astype(BF16), ffn1_w3[l].astype(BF16),
                 ffn1_w2[l].astype(BF16))
        proj = _proj(x, norm_mix[l], sh2, sc2, _permute_w_in(w_in[l], D))
        ya = _gdn(proj, gdn_conv[l], gdn_a_log[l], gdn_dt_bias[l], gdn_out_norm[l], off_aq, off_ba)
        yb = _s5(proj, s5_a_re[l], s5_a_im[l], s5_b_re[l], s5_b_im[l], s5_c_re[l], s5_c_im[l],
                 s5_d[l], s5_log_step[l], s5_glu_w[l], s5_glu_b[l])
        yc = _dil(proj, dil_q_norm[l], dil_k_norm[l])
        x = _merge(x, ya, yb, yc, proj, g2, w_branch_a[l].astype(BF16), w_branch_b[l].astype(BF16),
                   w_branch_c[l].astype(BF16), w_out[l].astype(BF16))
        x = _ffn(x, norm_ffn2[l], sh3, sc3, g3, ffn2_w1[l].astype(BF16), ffn2_w3[l].astype(BF16),
                 ffn2_w2[l].astype(BF16))
    return x
```

```python
import functools
import math

import jax
import jax.numpy as jnp
from jax import lax
from jax.experimental import pallas as pl
from jax.experimental.pallas import tpu as pltpu

F32 = jnp.float32
BF16 = jnp.bfloat16
EPS = 1e-6
NEG = -1e30

LANES = 128
V7X_VMEM_LIMIT_BYTES = 56 * 1024 * 1024

FFN_RES = 0.5
FFN_OUT_CHUNK = 512
FFN_NORM_ROWS = 256
N_MOD = 9
GDN_HEADS = 8
GDN_HEAD_DIM = 128
GDN_WIDTH = GDN_HEADS * GDN_HEAD_DIM
GDN_CONV = 4
GDN_CHUNK = 64
GDN_PHASE1_CHUNKS = 8
S5_GROUP = 16
S5_WIDTH = 768
S5_GROUPS = S5_WIDTH // S5_GROUP
S5_STATE = 64
S5_MAX_RE = -1e-4
S5_TILE_GROUPS = LANES // S5_GROUP
S5_TILES = S5_GROUPS // S5_TILE_GROUPS
S5_TILE_STATES = S5_TILE_GROUPS * S5_STATE
S5_SCAN_UNROLL = 4
DIL_PAIRS = ((128, 1), (512, 4), (2048, 16))
DIL_HEADS_PER_GROUP = 4
DIL_HEAD_DIM = 64
DIL_SUBHEADS = len(DIL_PAIRS) * DIL_HEADS_PER_GROUP
DIL_WIDTH = DIL_SUBHEADS * DIL_HEAD_DIM
DIL_OUT = DIL_HEADS_PER_GROUP * DIL_HEAD_DIM
DIL_BLK = 128
ALIBI_MAX = 8.0
N_BRANCH = 3

OFF_S5 = 0
OFF_CQ = OFF_S5 + S5_WIDTH
OFF_CK = OFF_CQ + DIL_WIDTH
OFF_CV = OFF_CK + DIL_WIDTH
OFF_Z = OFF_CV + DIL_WIDTH
OFF_GATES = OFF_Z + GDN_WIDTH
OFF_AQ = None


def _cparams(sem, vmem=V7X_VMEM_LIMIT_BYTES):
    return pltpu.CompilerParams(dimension_semantics=sem, vmem_limit_bytes=vmem)


def _dot(a, b, precision=None):
    return jnp.dot(a, b, preferred_element_type=F32, precision=precision)


def _dot_nt(a, b, precision=None):
    return lax.dot_general(a, b, (((1,), (1,)), ((), ())), preferred_element_type=F32,
                           precision=precision)


def _dot_tn(a, b, precision=None):
    return lax.dot_general(a, b, (((0,), (0,)), ((), ())), preferred_element_type=F32,
                           precision=precision)


def _silu(x):
    return x * jax.nn.sigmoid(x)


def _ada_kernel(c_ref, w_ref, b_ref, o_ref):
    c = c_ref[...]
    o_ref[0] = _dot(_silu(c), w_ref[0]) + b_ref[0]


def _ada_mod(c, ada_w, ada_b, tn=1024):
    L, D, N = ada_w.shape
    B = c.shape[0]
    return pl.pallas_call(
        _ada_kernel,
        grid=(L, N // tn),
        in_specs=[pl.BlockSpec((B, D), lambda l, j: (0, 0)),
                  pl.BlockSpec((1, D, tn), lambda l, j: (l, 0, j)),
                  pl.BlockSpec((1, 1, tn), lambda l, j: (l, 0, j))],
        out_specs=pl.BlockSpec((1, B, tn), lambda l, j: (l, 0, j)),
        out_shape=jax.ShapeDtypeStruct((L, B, N), F32),
        compiler_params=_cparams(("arbitrary", "arbitrary")),
        name="ada_mod",
    )(c, ada_w, ada_b.reshape(L, 1, N))


def _norm_mod(x, g, sh, sc):
    ms = jnp.mean(x * x, axis=-1, keepdims=True)
    y = x * lax.rsqrt(ms + EPS) * g
    return y * (1.0 + sc) + sh


def _ffn_kernel(x_ref, g_ref, sh_ref, sc_ref, gate_ref, w1_ref, w3_ref, w2_ref, o_ref, hn_ref):
    j = pl.program_id(1)

    @pl.when(j == 0)
    def _():
        rc = FFN_NORM_ROWS
        for r in range(x_ref.shape[1] // rc):
            rs = slice(r * rc, (r + 1) * rc)
            x = x_ref[0, rs, :]
            hn_ref[rs, :] = _norm_mod(x, g_ref[...], sh_ref[0], sc_ref[0]).astype(BF16)
            o_ref[0, rs, :] = x

    hn = hn_ref[...]
    h1 = _dot(hn, w1_ref[...])
    h3 = _dot(hn, w3_ref[...])
    a = (_silu(h1) * h3).astype(BF16)
    scale = FFN_RES * gate_ref[0]
    tn = FFN_OUT_CHUNK
    for n in range(o_ref.shape[-1] // tn):
        cs = slice(n * tn, (n + 1) * tn)
        o_ref[0, :, cs] += scale[:, cs] * _dot(a, w2_ref[:, cs])


def _ffn(x, g, sh, sc, gate, w1, w3, w2, tm=1024, tf=512):
    B, S, D = x.shape
    F = w1.shape[1]
    nS = S // tm
    xmap = lambda i, j: (i // nS, i % nS, 0)
    bmap = lambda i, j: (i // nS, 0, 0)
    return pl.pallas_call(
        _ffn_kernel,
        grid=(B * nS, F // tf),
        in_specs=[pl.BlockSpec((1, tm, D), xmap),
                  pl.BlockSpec((1, D), lambda i, j: (0, 0)),
                  pl.BlockSpec((1, 1, D), bmap),
                  pl.BlockSpec((1, 1, D), bmap),
                  pl.BlockSpec((1, 1, D), bmap),
                  pl.BlockSpec((D, tf), lambda i, j: (0, j)),
                  pl.BlockSpec((D, tf), lambda i, j: (0, j)),
                  pl.BlockSpec((tf, D), lambda i, j: (j, 0))],
        out_specs=pl.BlockSpec((1, tm, D), xmap),
        out_shape=jax.ShapeDtypeStruct((B, S, D), F32),
        scratch_shapes=[pltpu.VMEM((tm, D), BF16)],
        compiler_params=_cparams(("arbitrary", "arbitrary")),
        name="ffn",
    )(x, g.reshape(1, D), sh.reshape(B, 1, D), sc.reshape(B, 1, D), gate.reshape(B, 1, D),
      w1, w3, w2)


def _proj_kernel(x_ref, g_ref, sh_ref, sc_ref, w_ref, o_ref, hn_ref):
    @pl.when(pl.program_id(1) == 0)
    def _():
        hn_ref[...] = _norm_mod(x_ref[0], g_ref[...], sh_ref[0], sc_ref[0]).astype(BF16)

    o_ref[0] = _dot(hn_ref[...], w_ref[...])


def _proj(x, g, sh, sc, w, tm=1024, tn=1920):
    B, S, D = x.shape
    N = w.shape[1]
    nS = S // tm
    xmap = lambda i, j: (i // nS, i % nS, 0)
    bmap = lambda i, j: (i // nS, 0, 0)
    return pl.pallas_call(
        _proj_kernel,
        grid=(B * nS, N // tn),
        in_specs=[pl.BlockSpec((1, tm, D), xmap, pipeline_mode=pl.Buffered(1)),
                  pl.BlockSpec((1, D), lambda i, j: (0, 0)),
                  pl.BlockSpec((1, 1, D), bmap),
                  pl.BlockSpec((1, 1, D), bmap),
                  pl.BlockSpec((D, tn), lambda i, j: (0, j))],
        out_specs=pl.BlockSpec((1, tm, tn), lambda i, j: (i // nS, i % nS, j)),
        out_shape=jax.ShapeDtypeStruct((B, S, N), F32),
        scratch_shapes=[pltpu.VMEM((tm, D), BF16)],
        compiler_params=_cparams(("arbitrary", "arbitrary")),
        name="in_proj",
    )(x, g.reshape(1, D), sh.reshape(B, 1, D), sc.reshape(B, 1, D), w)


def _gdn_kernel(q_ref, k_ref, v_ref, z_ref, ba_ref, cq_ref, ck_ref, cv_ref, al_ref, dtb_ref,
                on_ref, o_ref, qn_s, kn_s, vv_s, qo_s, ku_s, kp_s, cols_s, gct_s,
                *, S, HG):
    C = GDN_CHUNK
    HD = GDN_HEAD_DIM
    hg = pl.program_id(1)
    NH = GDN_HEADS
    shift = (LANES - hg * HG) % LANES
    t16 = pltpu.roll(ba_ref[0], shift, 1).T[0:2 * NH, :]
    a_col = al_ref[0][:, 0:1]
    d_col = dtb_ref[0][:, 0:1]
    beta16 = jax.nn.sigmoid(t16)
    xg = t16 + d_col
    g16 = -jnp.exp(a_col) * (jnp.maximum(xg, 0.0) + jnp.log1p(jnp.exp(-jnp.abs(xg))))
    lanec = lax.broadcasted_iota(jnp.int32, (2 * NH, S), 1) % C
    for d in (1, 2, 4, 8, 16, 32):
        g16 = g16 + jnp.where(lanec >= d, pltpu.roll(g16, d, 1), 0.0)
    gl16 = jnp.where(lanec == C - 1, g16, 0.0)
    for d in (1, 2, 4, 8, 16, 32):
        gl16 = gl16 + jnp.where(lanec + d < C, pltpu.roll(gl16, S - d, 1), 0.0)
    gct_s[...] = g16[NH:2 * NH, :]
    cols_s[...] = jnp.concatenate(
        [beta16[0:NH], g16[NH:2 * NH], gl16[NH:2 * NH], jnp.zeros((LANES - 3 * NH, S), F32)], axis=0).T

    SUB = 8
    row8 = lax.broadcasted_iota(jnp.int32, (SUB, HD), 0)

    def conv_silu(x_ref, w_ref, sl):
        w = w_ref[:, sl]
        K = GDN_CONV
        x0 = x_ref[0, 0:SUB, sl]
        head = x0 * w[K - 1:K, :]
        body = x_ref[0, SUB:S, sl] * w[K - 1:K, :]
        for kk in range(K - 1):
            s = K - 1 - kk
            head = head + jnp.where(row8 >= s, pltpu.roll(x0, s, 0), 0.0) * w[kk:kk + 1, :]
            body = body + x_ref[0, SUB - s:S - s, sl] * w[kk:kk + 1, :]
        return _silu(jnp.concatenate([head, body], axis=0))

    def l2n(x):
        return x * lax.rsqrt(jnp.sum(x * x, axis=-1, keepdims=True) + EPS)

    for j in range(HG):
        sl = slice(j * HD, (j + 1) * HD)
        qn_s[:, sl] = l2n(conv_silu(q_ref, cq_ref, sl)) * (HD ** -0.5)
        kn_s[:, sl] = l2n(conv_silu(k_ref, ck_ref, sl))
        vv_s[:, sl] = conv_silu(v_ref, cv_ref, sl)

    ii = lax.broadcasted_iota(jnp.int32, (C, C), 0)
    jj = lax.broadcasted_iota(jnp.int32, (C, C), 1)
    tril = ii >= jj
    strict = ii > jj
    eye = jnp.where(ii == jj, 1.0, 0.0)
    bd = {w: (ii // w) == (jj // w) for w in (8, 16, 32)}
    base_mask = strict & bd[8]
    merge_masks = (bd[16] & ~bd[8], bd[32] & ~bd[16], ~bd[32])
    NC = GDN_PHASE1_CHUNKS

    def phase1(p, carry):
        r0 = pl.multiple_of(p * NC * C, NC * C)
        gct = gct_s[:, pl.ds(r0, NC * C)]
        chains = []
        for cc in range(NC):
            rows = pl.ds(r0 + cc * C, C)
            cols = cols_s[rows, :]
            for j in range(HG):
                sl = slice(j * HD, (j + 1) * HD)
                gcol = cols[:, NH + j:NH + j + 1]
                glcol = cols[:, 2 * NH + j:2 * NH + j + 1]
                grow = gct[j:j + 1, cc * C:(cc + 1) * C]
                bcol = cols[:, j:j + 1]
                kn_c = kn_s[rows, sl]
                kb = kn_c * bcol
                eg = jnp.exp(gcol)
                chains.append(dict(
                    cc=cc, j=j, rows=rows, kn=kn_c, kb=kb, qn=qn_s[rows, sl], eg=eg,
                    egl=jnp.exp(glcol[0:1, :]),
                    decay=jnp.where(tril, jnp.exp(jnp.minimum(gcol - grow, 0.0)), 0.0),
                    rhs=jnp.concatenate([vv_s[rows, sl] * bcol, kb * eg], axis=1),
                    kdec=kn_c * jnp.exp(glcol - gcol)))
        for ch in chains:
            ch['lm'] = jnp.where(strict, _dot_nt(ch['kb'], ch['kn']) * ch['decay'], 0.0)
        for ch in chains:
            ch['attn'] = _dot_nt(ch['qn'], ch['kn']) * ch['decay']
        for ch in chains:
            ch['l8'] = jnp.where(base_mask, ch['lm'], 0.0)
            ch['s2'] = _dot(ch['l8'], ch['l8'])
        for ch in chains:
            n1 = eye - ch['l8']
            ch['n2'] = n1 + _dot(n1, ch['s2'])
        for ch in chains:
            ch['s4'] = _dot(ch['s2'], ch['s2'])
        for ch in chains:
            ch['t'] = ch['n2'] + _dot(ch['n2'], ch['s4'])
        for mm in merge_masks:
            for ch in chains:
                ch['tl'] = _dot(ch['t'], jnp.where(mm, ch['lm'], 0.0))
            for ch in chains:
                ch['t'] = ch['t'] - _dot(ch['tl'], ch['t'])
        for ch in chains:
            ch['x'] = _dot(ch['t'], ch['rhs'])
        for ch in chains:
            au = _dot(ch['attn'], ch['x'])
            j = ch['j']
            qo_s[ch['rows'], j * 2 * HD:(j + 1) * 2 * HD] = jnp.concatenate(
                [au[:, :HD], ch['qn'] * ch['eg'] - au[:, HD:]], axis=1)
        for ch in chains:
            ch['ku'] = _dot_tn(ch['kdec'], ch['x'])
        pairs = [(chains[cp * HG + j], chains[(cp + 1) * HG + j])
                 for cp in range(0, NC, 2) for j in range(HG)]
        for ch0, ch1 in pairs:
            j = ch0['j']
            pr = _dot(ch1['ku'][:, HD:], ch0['ku'])
            e0, e1 = ch0['egl'], ch1['egl']
            b2 = e1 * ch0['ku'][:, :HD] - pr[:, :HD] + ch1['ku'][:, :HD]
            m2 = e1 * ch0['ku'][:, HD:] + e0 * ch1['ku'][:, HD:] - pr[:, HD:]
            pair = p * (NC // 2) + ch0['cc'] // 2
            ku_s[pair, j] = ch0['ku']
            kp_s[pair, j] = jnp.concatenate([b2, m2], axis=1)
        return carry

    lax.fori_loop(0, S // (NC * C), phase1, 0)

    def gated_out(o, rows, sl):
        on = o * lax.rsqrt(jnp.mean(o * o, axis=-1, keepdims=True) + EPS) * on_ref[...]
        zc = z_ref[0, rows, sl]
        o_ref[0, rows, sl] = (on * _silu(zc)).astype(o_ref.dtype)

    def phase2(pi, hs):
        r0 = pl.multiple_of(pi * 2 * C, 2 * C)
        rows_a = pl.ds(r0, C)
        rows_b = pl.ds(r0 + C, C)
        gla = cols_s[pl.ds(r0, SUB), :]
        glb = cols_s[pl.ds(r0 + C, SUB), :]
        st = []
        for j in range(HG):
            ku = ku_s[pi, j]
            kp = kp_s[pi, j]
            qo_a = qo_s[rows_a, j * 2 * HD:(j + 1) * 2 * HD]
            r = _dot(jnp.concatenate([kp[:, HD:], ku[:, HD:], qo_a[:, HD:]], axis=0), hs[j])
            st.append((ku, kp, qo_a, r))
        new_hs = []
        for j in range(HG):
            sl = slice(j * HD, (j + 1) * HD)
            ku, kp, qo_a, r = st[j]
            h = hs[j]
            qo_b = qo_s[rows_b, j * 2 * HD:(j + 1) * 2 * HD]
            e0 = jnp.exp(gla[0:1, 2 * NH + j:2 * NH + j + 1])
            e1 = jnp.exp(glb[0:1, 2 * NH + j:2 * NH + j + 1])
            new_hs.append(h * (e0 * e1) - r[:HD] + kp[:, :HD])
            h_b = h * e0 - r[HD:2 * HD] + ku[:, :HD]
            gated_out(r[2 * HD:] + qo_a[:, :HD], rows_a, sl)
            gated_out(_dot(qo_b[:, HD:], h_b) + qo_b[:, :HD], rows_b, sl)
        return tuple(new_hs)

    h0 = tuple(jnp.zeros((HD, HD), F32) for _ in range(HG))
    lax.fori_loop(0, S // (2 * C), phase2, h0)


def _gdn(proj, conv_w, a_log, dt_bias, out_norm, off_aq, off_ba, HG=2):
    B, S, _ = proj.shape
    W = HG * GDN_HEAD_DIM
    nG = GDN_HEADS // HG

    def pad(v):
        rows = jnp.zeros((nG, 2 * GDN_HEADS), F32).at[:, GDN_HEADS:GDN_HEADS + HG].set(v.reshape(nG, HG))
        return jnp.broadcast_to(rows[:, :, None], (nG, 2 * GDN_HEADS, LANES))

    kern = functools.partial(_gdn_kernel, S=S, HG=HG)
    par = pl.BlockSpec((1, 2 * GDN_HEADS, LANES), lambda b, h: (h, 0, 0))
    col = lambda base: (lambda b, h: (b, 0, base // W + h))
    cw = lambda base: (lambda b, h: (0, base // W + h))
    one = lambda b, h: (0, 0)
    return pl.pallas_call(
        kern,
        grid=(B, nG),
        in_specs=[pl.BlockSpec((1, S, W), col(off_aq)),
                  pl.BlockSpec((1, S, W), col(off_aq + GDN_WIDTH)),
                  pl.BlockSpec((1, S, W), col(off_aq + 2 * GDN_WIDTH)),
                  pl.BlockSpec((1, S, W), col(OFF_Z)),
                  pl.BlockSpec((1, S, LANES), lambda b, h: (b, 0, off_ba // LANES)),
                  pl.BlockSpec((GDN_CONV, W), cw(0)),
                  pl.BlockSpec((GDN_CONV, W), cw(GDN_WIDTH)),
                  pl.BlockSpec((GDN_CONV, W), cw(2 * GDN_WIDTH)),
                  par, par,
                  pl.BlockSpec((1, GDN_HEAD_DIM), one)],
        out_specs=pl.BlockSpec((1, S, W), lambda b, h: (b, 0, h)),
        out_shape=jax.ShapeDtypeStruct((B, S, GDN_WIDTH), BF16),
        scratch_shapes=[pltpu.VMEM((S, W), F32), pltpu.VMEM((S, W), F32), pltpu.VMEM((S, W), F32),
                        pltpu.VMEM((S, 2 * W), F32),
                        pltpu.VMEM((S // (2 * GDN_CHUNK), HG, GDN_HEAD_DIM, 2 * GDN_HEAD_DIM), F32),
                        pltpu.VMEM((S // (2 * GDN_CHUNK), HG, GDN_HEAD_DIM, 2 * GDN_HEAD_DIM), F32),
                        pltpu.VMEM((S, LANES), F32), pltpu.VMEM((GDN_HEADS, S), F32)],
        compiler_params=_cparams(("arbitrary", "arbitrary")),
        name="gdn",
    )(proj, proj, proj, proj, proj, conv_w, conv_w, conv_w, pad(a_log), pad(dt_bias),
      out_norm.reshape(1, GDN_HEAD_DIM))


def _s5_disc_kernel(are_ref, aim_ref, ls_ref, bre_ref, bim_ref, lr_ref, li_ref, br_ref, bi_ref):
    lam_re = jnp.minimum(are_ref[...], S5_MAX_RE)
    lam_im = aim_ref[...]
    step = jnp.exp(ls_ref[...])
    mag = jnp.exp(lam_re * step)
    lbr = mag * jnp.cos(lam_im * step)
    lbi = mag * jnp.sin(lam_im * step)
    lr_ref[...] = lbr
    li_ref[...] = lbi
    nr = lbr - 1.0
    den = lam_re * lam_re + lam_im * lam_im
    cr = (nr * lam_re + lbi * lam_im) / den
    ci = (lbi * lam_re - nr * lam_im) / den
    bre = bre_ref[...]
    bim = bim_ref[...]
    br_ref[...] = cr * bre - ci * bim
    bi_ref[...] = cr * bim + ci * bre


def _s5_discretize(a_re, a_im, log_step, b_re, b_im):
    G, P = a_re.shape
    I = b_re.shape[-1]
    gp = jax.ShapeDtypeStruct((G, 1, P), F32)
    gip = jax.ShapeDtypeStruct((G, I, P), F32)
    return pl.pallas_call(
        _s5_disc_kernel, out_shape=(gp, gp, gip, gip), name="s5_discretize",
    )(a_re.reshape(G, 1, P), a_im.reshape(G, 1, P), log_step.reshape(G, 1, 1),
      jnp.swapaxes(b_re, 1, 2), jnp.swapaxes(b_im, 1, 2))


def _s5_kernel(u_ref, pf_ref, pb_ref, bb_ref, lr_ref, li_ref, cc_ref, d_ref, gw_ref, gb_ref,
               o_ref, xr_s, xi_s, cr_s, ci_s, y_s, *, B, tt):
    TS = S5_TILE_STATES
    R = B * tt

    @pl.when(pl.program_id(0) == 0)
    def _():
        cr_s[...] = jnp.zeros_like(cr_s)
        ci_s[...] = jnp.zeros_like(ci_s)

    u = u_ref[...].reshape(R, S5_WIDTH)
    u_hi = u.astype(BF16)
    u_lo = (u - u_hi.astype(F32)).astype(BF16)
    up = _dot(pf_ref[...], jnp.concatenate([u_hi, u_lo], axis=1))
    u_tb = up[:, :S5_WIDTH] + up[:, S5_WIDTH:]
    for t in range(S5_TILES):
        ut = up[:, t * LANES:(t + 1) * LANES].astype(BF16)
        bu = _dot(ut, bb_ref[t])
        xr_s[:, :, t * TS:(t + 1) * TS] = bu[:, :TS].reshape(tt, B, TS)
        xi_s[:, :, t * TS:(t + 1) * TS] = bu[:, TS:].reshape(tt, B, TS)

    for t in range(S5_TILES):
        ls = slice(t * TS, (t + 1) * TS)
        ar = lr_ref[:, ls]
        ai = li_ref[:, ls]

        def step(s, carry, ls=ls, ar=ar, ai=ai):
            xr, xi = carry
            nr = ar * xr - ai * xi + xr_s[s, :, ls]
            ni = ar * xi + ai * xr + xi_s[s, :, ls]
            xr_s[s, :, ls] = nr
            xi_s[s, :, ls] = ni
            return nr, ni

        xr, xi = lax.fori_loop(0, tt, step, (cr_s[:, ls], ci_s[:, ls]), unroll=S5_SCAN_UNROLL)
        cr_s[:, ls] = xr
        ci_s[:, ls] = xi

    for t in range(S5_TILES):
        ls = slice(t * TS, (t + 1) * TS)
        lo = slice(t * LANES, (t + 1) * LANES)
        x = jnp.concatenate([xr_s[:, :, ls].reshape(R, TS), xi_s[:, :, ls].reshape(R, TS)], axis=1)
        y = _dot(x.astype(BF16), cc_ref[t]) + d_ref[:, lo] * u_tb[:, lo]
        y_s[:, lo] = jax.nn.gelu(y).astype(BF16)
    z = _dot(y_s[...], gw_ref[...]) + gb_ref[...]
    out = (z[:, :S5_WIDTH] * jax.nn.sigmoid(z[:, S5_WIDTH:])).astype(BF16)
    o_ref[...] = _dot(pb_ref[...], out).reshape(B, tt, S5_WIDTH).astype(o_ref.dtype)


def _blockdiag_in(m):
    t = m.reshape(S5_TILES, S5_TILE_GROUPS, S5_GROUP, S5_STATE)
    eye = jnp.eye(S5_TILE_GROUPS, dtype=m.dtype)
    return jnp.einsum('tgip,gh->tgihp', t, eye).reshape(S5_TILES, LANES, S5_TILE_STATES)


def _blockdiag_out(m):
    t = m.reshape(S5_TILES, S5_TILE_GROUPS, S5_GROUP, S5_STATE)
    eye = jnp.eye(S5_TILE_GROUPS, dtype=m.dtype)
    return jnp.einsum('tgip,gh->tgphi', t, eye).reshape(S5_TILES, S5_TILE_STATES, LANES)


def _s5(proj, a_re, a_im, b_re, b_im, c_re, c_im, d_skip, log_step, glu_w, glu_b, tt=32):
    B, S, _ = proj.shape
    lr, li, bbr, bbi = _s5_discretize(a_re, a_im, log_step, b_re, b_im)
    NS = S5_GROUPS * S5_STATE
    kern = functools.partial(_s5_kernel, B=B, tt=tt)
    R = B * tt
    src = (jnp.arange(R) % B) * tt + jnp.arange(R) // B
    pf = (src[:, None] == jnp.arange(R)[None, :]).astype(BF16)
    bb = jnp.concatenate([_blockdiag_in(bbr), _blockdiag_in(bbi)], axis=2).astype(BF16)
    cc = jnp.concatenate([_blockdiag_out(c_re), -_blockdiag_out(c_im)], axis=1).astype(BF16)
    full2 = lambda i: (0, 0)
    full3 = lambda i: (0, 0, 0)
    return pl.pallas_call(
        kern,
        grid=(S // tt,),
        in_specs=[pl.BlockSpec((B, tt, S5_WIDTH), lambda i: (0, i, OFF_S5 // S5_WIDTH)),
                  pl.BlockSpec((R, R), full2),
                  pl.BlockSpec((R, R), full2),
                  pl.BlockSpec((S5_TILES, LANES, 2 * S5_TILE_STATES), full3),
                  pl.BlockSpec((1, NS), full2),
                  pl.BlockSpec((1, NS), full2),
                  pl.BlockSpec((S5_TILES, 2 * S5_TILE_STATES, LANES), full3),
                  pl.BlockSpec((1, S5_WIDTH), full2),
                  pl.BlockSpec((S5_WIDTH, 2 * S5_WIDTH), full2),
                  pl.BlockSpec((1, 2 * S5_WIDTH), full2)],
        out_specs=pl.BlockSpec((B, tt, S5_WIDTH), lambda i: (0, i, 0)),
        out_shape=jax.ShapeDtypeStruct((B, S, S5_WIDTH), BF16),
        scratch_shapes=[pltpu.VMEM((tt, B, NS), F32), pltpu.VMEM((tt, B, NS), F32),
                        pltpu.VMEM((B, NS), F32), pltpu.VMEM((B, NS), F32),
                        pltpu.VMEM((R, S5_WIDTH), BF16)],
        compiler_params=_cparams(("arbitrary",)),
        name="s5",
    )(proj, pf, pf.T, bb, lr.reshape(1, NS), li.reshape(1, NS), cc,
      d_skip.reshape(1, S5_WIDTH), glu_w.astype(BF16), glu_b.reshape(1, 2 * S5_WIDTH))


def _dil_kernel(q0, q1, q2, k0, k1, k2, v0, v1, v2, qg_ref, kg_ref, sl_ref, o_ref,
                qs, ks, vs, og, mg, lg, oacc, macc, lacc, *, S):
    q_refs, k_refs, v_refs = (q0, q1, q2), (k0, k1, k2), (v0, v1, v2)
    BLK = DIL_BLK
    HD = DIL_HEAD_DIM
    lane = lax.broadcasted_iota(jnp.int32, (1, LANES), 1)
    first = lane < HD
    ii = lax.broadcasted_iota(jnp.int32, (BLK, BLK), 0)
    jj = lax.broadcasted_iota(jnp.int32, (BLK, BLK), 1)
    steps_cur = (ii - jj).astype(F32)
    steps_prev = (ii - jj + BLK).astype(F32)
    ok_cur = ii >= jj
    ok_prev = jj >= ii

    li = lax.broadcasted_iota(jnp.int32, (LANES, LANES), 0) // HD
    lj = lax.broadcasted_iota(jnp.int32, (LANES, LANES), 1) // HD
    avg = jnp.where(li == lj, 1.0 / HD, 0.0).astype(BF16)

    def rms_pair(x, gain):
        x2 = x * x
        hi = x2.astype(BF16)
        lo = (x2 - hi.astype(F32)).astype(BF16)
        ms = _dot(hi, avg) + _dot(lo, avg)
        return x * lax.rsqrt(ms + EPS) * gain

    for g, (window, d) in enumerate(DIL_PAIRS):
        sd = S // d
        nb = sd // BLK
        for r in range(d):
            src = pl.ds(r, sd, stride=d) if d > 1 else slice(None)
            dst = slice(r * sd, (r + 1) * sd)
            qs[dst, :] = rms_pair(q_refs[g][0, src, :], qg_ref[...]) * (HD ** -0.5)
            ks[dst, :] = rms_pair(k_refs[g][0, src, :], kg_ref[...])
            vs[dst, :] = v_refs[g][0, src, :]

        sl_a = sl_ref[0, g:g + 1, 0:1] * float(d)
        sl_b = sl_ref[0, g:g + 1, HD:HD + 1] * float(d)
        bias = [(jnp.where(ok_cur, -s * steps_cur, NEG), jnp.where(ok_prev, -s * steps_prev, NEG))
                for s in (sl_a, sl_b)]

        def block_pair(ip, carry, nb=nb, bias=bias):
            st0 = pl.multiple_of(ip * 2 * BLK, 2 * BLK)
            ch = []
            for bi in range(2):
                rows = pl.ds(st0 + bi * BLK, BLK)
                blk = dict(rows=rows, q=qs[rows, :], kc=ks[rows, :], vc=vs[rows, :])
                if nb > 1 and bi == 0:
                    stp = pl.multiple_of(jnp.maximum(st0 - BLK, 0), BLK)
                    blk.update(kp=ks[pl.ds(stp, BLK), :], vp=vs[pl.ds(stp, BLK), :],
                               pen=jnp.where(((2 * ip) % nb) > 0, 0.0, NEG))
                elif nb > 1:
                    blk.update(kp=ch[0]['blk']['kc'], vp=ch[0]['blk']['vc'], pen=0.0)
                for hh in range(2):
                    qh = jnp.where(first, blk['q'], 0.0) if hh == 0 else jnp.where(first, 0.0, blk['q'])
                    ch.append(dict(blk=blk, hh=hh, qh=qh))
            for c in ch:
                c['sc'] = _dot_nt(c['qh'], c['blk']['kc']) + bias[c['hh']][0]
            if nb > 1:
                for c in ch:
                    c['sp'] = _dot_nt(c['qh'], c['blk']['kp']) + (bias[c['hh']][1] + c['blk']['pen'])
            for c in ch:
                m = jnp.max(c['sc'], axis=-1, keepdims=True)
                if nb > 1:
                    m = jnp.maximum(m, jnp.max(c['sp'], axis=-1, keepdims=True))
                c['m'] = m
                c['pc'] = jnp.exp(c['sc'] - m)
                c['l'] = jnp.sum(c['pc'], axis=-1, keepdims=True)
                if nb > 1:
                    c['pp'] = jnp.exp(c['sp'] - m)
                    c['l'] = c['l'] + jnp.sum(c['pp'], axis=-1, keepdims=True)
            for c in ch:
                c['o'] = _dot(c['pc'], c['blk']['vc'])
            if nb > 1:
                for c in ch:
                    c['o'] = c['o'] + _dot(c['pp'], c['blk']['vp'])
            for bi in range(2):
                a, b = ch[2 * bi], ch[2 * bi + 1]
                rows = a['blk']['rows']
                og[rows, :] = jnp.where(first, a['o'], b['o'])
                mg[rows, :] = jnp.where(first, a['m'], b['m'])
                lg[rows, :] = jnp.where(first, a['l'], b['l'])
            return carry

        lax.fori_loop(0, S // (2 * BLK), block_pair, 0)

        if g == 0:
            oacc[...] = og[...]
            macc[...] = mg[...]
            lacc[...] = lg[...]
        else:
            for r in range(d):
                tok = pl.ds(r, sd, stride=d)
                res = slice(r * sd, (r + 1) * sd)
                m_old = macc[tok, :]
                m_g = mg[res, :]
                m_new = jnp.maximum(m_old, m_g)
                a = jnp.exp(m_old - m_new)
                b = jnp.exp(m_g - m_new)
                lacc[tok, :] = a * lacc[tok, :] + b * lg[res, :]
                oacc[tok, :] = a * oacc[tok, :] + b * og[res, :]
                macc[tok, :] = m_new

    o_ref[0] = (oacc[...] / lacc[...]).astype(o_ref.dtype)


def _dil(proj, q_norm, k_norm):
    B, S, _ = proj.shape
    sub = jnp.arange(1, DIL_SUBHEADS + 1, dtype=F32)
    slopes = jnp.power(2.0, -ALIBI_MAX * sub / DIL_SUBHEADS)
    sl = slopes.reshape(len(DIL_PAIRS), 2, 2).transpose(1, 0, 2)
    sl = jnp.repeat(sl, DIL_HEAD_DIM, axis=2)
    tile2 = lambda v: jnp.tile(v.reshape(1, DIL_HEAD_DIM), (1, 2))
    kern = functools.partial(_dil_kernel, S=S)
    col = lambda base, g: (lambda b, hp: (b, 0, base // LANES + 2 * g + hp))
    specs = [pl.BlockSpec((1, S, LANES), col(base, g))
             for base in (OFF_CQ, OFF_CK, OFF_CV) for g in range(len(DIL_PAIRS))]
    one = lambda b, hp: (0, 0)
    seq = lambda: pltpu.VMEM((S, LANES), F32)
    return pl.pallas_call(
        kern,
        grid=(B, 2),
        in_specs=specs + [pl.BlockSpec((1, LANES), one), pl.BlockSpec((1, LANES), one),
                          pl.BlockSpec((1, len(DIL_PAIRS), LANES), lambda b, hp: (hp, 0, 0))],
        out_specs=pl.BlockSpec((1, S, LANES), lambda b, hp: (b, 0, hp)),
        out_shape=jax.ShapeDtypeStruct((B, S, DIL_OUT), BF16),
        scratch_shapes=[seq() for _ in range(9)],
        compiler_params=_cparams(("arbitrary", "arbitrary")),
        name="dilated_attn",
    )(*([proj] * 9), tile2(q_norm), tile2(k_norm), sl)


def _merge_kernel(x_ref, ya_ref, yb_ref, yc_ref, g0_ref, g1_ref, g2_ref, gm_ref,
                  wa_ref, wb_ref, wc_ref, wo_ref, o_ref):
    merged = (jax.nn.sigmoid(g0_ref[0]) * _dot(ya_ref[0], wa_ref[...])
              + jax.nn.sigmoid(g1_ref[0]) * _dot(yb_ref[0], wb_ref[...])
              + jax.nn.sigmoid(g2_ref[0]) * _dot(yc_ref[0], wc_ref[...]))
    out = _dot(merged.astype(BF16), wo_ref[...])
    o_ref[0] = x_ref[0] + gm_ref[0] * out


def _merge(x, ya, yb, yc, proj, gmod, wa, wb, wc, wo, tm=256):
    B, S, D = x.shape
    nS = S // tm
    rmap = lambda i: (i // nS, i % nS, 0)
    gate = lambda k: (lambda i: (i // nS, i % nS, OFF_GATES // D + k))
    const = lambda i: (0, 0)
    wspec = lambda w: pl.BlockSpec(w.shape, const, pipeline_mode=pl.Buffered(1))
    return pl.pallas_call(
        _merge_kernel,
        grid=(B * nS,),
        in_specs=[pl.BlockSpec((1, tm, D), rmap),
                  pl.BlockSpec((1, tm, ya.shape[-1]), rmap),
                  pl.BlockSpec((1, tm, yb.shape[-1]), rmap),
                  pl.BlockSpec((1, tm, yc.shape[-1]), rmap),
                  pl.BlockSpec((1, tm, D), gate(0)),
                  pl.BlockSpec((1, tm, D), gate(1)),
                  pl.BlockSpec((1, tm, D), gate(2)),
                  pl.BlockSpec((1, 1, D), lambda i: (i // nS, 0, 0)),
                  wspec(wa), wspec(wb), wspec(wc), wspec(wo)],
        out_specs=pl.BlockSpec((1, tm, D), rmap),
        out_shape=jax.ShapeDtypeStruct((B, S, D), F32),
        compiler_params=_cparams(("arbitrary",)),
        name="merge_out",
    )(x, ya, yb, yc, proj, proj, proj, gmod.reshape(B, 1, D), wa, wb, wc, wo)


def _permute_w_in(w_in, D):
    o = 0
    aq = w_in[:, o:o + 3 * GDN_WIDTH]; o += 3 * GDN_WIDTH
    az = w_in[:, o:o + GDN_WIDTH]; o += GDN_WIDTH
    ba = w_in[:, o:o + 2 * GDN_HEADS]; o += 2 * GDN_HEADS
    bu = w_in[:, o:o + S5_WIDTH]; o += S5_WIDTH
    cq = w_in[:, o:o + 3 * DIL_WIDTH]; o += 3 * DIL_WIDTH
    gt = w_in[:, o:o + N_BRANCH * D]
    pad = jnp.zeros((w_in.shape[0], LANES - 2 * GDN_HEADS), w_in.dtype)
    return jnp.concatenate([bu, cq, az, gt, aq, ba, pad], axis=1).astype(BF16)


def kernel(x, c, ada_w, ada_b, norm_ffn1, ffn1_w1, ffn1_w3, ffn1_w2, norm_mix, w_in, gdn_conv, gdn_a_log, gdn_dt_bias, gdn_out_norm, s5_a_re, s5_a_im, s5_b_re, s5_b_im, s5_c_re, s5_c_im, s5_d, s5_log_step, s5_glu_w, s5_glu_b, dil_q_norm, dil_k_norm, w_branch_a, w_branch_b, w_branch_c, w_out, norm_ffn2, ffn2_w1, ffn2_w3, ffn2_w2):
    B, S, D = x.shape
    L = ada_w.shape[0]
    off_aq = OFF_GATES + N_BRANCH * D
    off_ba = off_aq + 3 * GDN_WIDTH
    mods = _ada_mod(c, ada_w, ada_b)
    for l in range(L):
        sh1, sc1, g1, sh2, sc2, g2, sh3, sc3, g3 = [mods[l, :, i * D:(i + 1) * D] for i in range(N_MOD)]
        x = _ffn(x, norm_ffn1[l], sh1, sc1, g1, ffn1_w1[l].astype(BF16), ffn1_w3[l].astype(BF16),
                 ffn1_w2[l].astype(BF16))
        proj = _proj(x, norm_mix[l], sh2, sc2, _permute_w_in(w_in[l], D))
        ya = _gdn(proj, gdn_conv[l], gdn_a_log[l], gdn_dt_bias[l], gdn_out_norm[l], off_aq, off_ba)
        yb = _s5(proj, s5_a_re[l], s5_a_im[l], s5_b_re[l], s5_b_im[l], s5_c_re[l], s5_c_im[l],
                 s5_d[l], s5_log_step[l], s5_glu_w[l], s5_glu_b[l])
        yc = _dil(proj, dil_q_norm[l], dil_k_norm[l])
        x = _merge(x, ya, yb, yc, proj, g2, w_branch_a[l].astype(BF16), w_branch_b[l].astype(BF16),
                   w_branch_c[l].astype(BF16), w_out[l].astype(BF16))
        x = _ffn(x, norm_ffn2[l], sh3, sc3, g3, ffn2_w1[l].astype(BF16), ffn2_w3[l].astype(BF16),
                 ffn2_w2[l].astype(BF16))
    return x
```

```python
import functools
import math

import jax
import jax.numpy as jnp
from jax import lax
from jax.experimental import pallas as pl
from jax.experimental.pallas import tpu as pltpu

F32 = jnp.float32
BF16 = jnp.bfloat16
EPS = 1e-6
NEG = -1e30

LANES = 128
V7X_VMEM_LIMIT_BYTES = 56 * 1024 * 1024

FFN_RES = 0.5
FFN_OUT_CHUNK = 512
FFN_NORM_ROWS = 256
PROJ_COL_TILE = 2304
N_MOD = 9
GDN_HEADS = 8
GDN_HEAD_DIM = 128
GDN_WIDTH = GDN_HEADS * GDN_HEAD_DIM
GDN_CONV = 4
GDN_CHUNK = 64
GDN_PHASE1_CHUNKS = 8
S5_GROUP = 16
S5_WIDTH = 768
S5_GROUPS = S5_WIDTH // S5_GROUP
S5_STATE = 64
S5_MAX_RE = -1e-4
S5_TILE_GROUPS = LANES // S5_GROUP
S5_TILES = S5_GROUPS // S5_TILE_GROUPS
S5_TILE_STATES = S5_TILE_GROUPS * S5_STATE
S5_SCAN_UNROLL = 4
DIL_PAIRS = ((128, 1), (512, 4), (2048, 16))
DIL_HEADS_PER_GROUP = 4
DIL_HEAD_DIM = 64
DIL_SUBHEADS = len(DIL_PAIRS) * DIL_HEADS_PER_GROUP
DIL_WIDTH = DIL_SUBHEADS * DIL_HEAD_DIM
DIL_OUT = DIL_HEADS_PER_GROUP * DIL_HEAD_DIM
DIL_BLK = 128
ALIBI_MAX = 8.0
N_BRANCH = 3

OFF_S5 = 0
OFF_CQ = OFF_S5 + S5_WIDTH
OFF_CK = OFF_CQ + DIL_WIDTH
OFF_CV = OFF_CK + DIL_WIDTH
OFF_Z = OFF_CV + DIL_WIDTH
OFF_GATES = OFF_Z + GDN_WIDTH
OFF_AQ = None


def _cparams(sem, vmem=V7X_VMEM_LIMIT_BYTES):
    return pltpu.CompilerParams(dimension_semantics=sem, vmem_limit_bytes=vmem)


def _mxu(x):
    return x if x.dtype == BF16 else x.astype(BF16)


def _dot(a, b):
    return jnp.dot(_mxu(a), _mxu(b), preferred_element_type=F32)


def _dot_nt(a, b):
    return lax.dot_general(_mxu(a), _mxu(b), (((1,), (1,)), ((), ())), preferred_element_type=F32)


def _dot_tn(a, b):
    return lax.dot_general(_mxu(a), _mxu(b), (((0,), (0,)), ((), ())), preferred_element_type=F32)


def _silu(x):
    return x * jax.nn.sigmoid(x)


def _ada_kernel(c_ref, w_ref, b_ref, o_ref):
    c = c_ref[...]
    o_ref[0] = _dot(_silu(c), w_ref[0]) + b_ref[0]


def _ada_mod(c, ada_w, ada_b, tn=1024):
    L, D, N = ada_w.shape
    B = c.shape[0]
    return pl.pallas_call(
        _ada_kernel,
        grid=(L, N // tn),
        in_specs=[pl.BlockSpec((B, D), lambda l, j: (0, 0)),
                  pl.BlockSpec((1, D, tn), lambda l, j: (l, 0, j)),
                  pl.BlockSpec((1, 1, tn), lambda l, j: (l, 0, j))],
        out_specs=pl.BlockSpec((1, B, tn), lambda l, j: (l, 0, j)),
        out_shape=jax.ShapeDtypeStruct((L, B, N), F32),
        compiler_params=_cparams(("arbitrary", "arbitrary")),
        name="ada_mod",
    )(c, ada_w, ada_b.reshape(L, 1, N))


def _norm_mod(x, g, sh, sc):
    ms = jnp.mean(x * x, axis=-1, keepdims=True)
    y = x * lax.rsqrt(ms + EPS) * g
    return y * (1.0 + sc) + sh


def _ffn_kernel(x_ref, g_ref, sh_ref, sc_ref, gate_ref, w1_ref, w3_ref, w2_ref, o_ref, hn_ref):
    j = pl.program_id(1)

    @pl.when(j == 0)
    def _():
        rc = FFN_NORM_ROWS
        for r in range(x_ref.shape[1] // rc):
            rs = slice(r * rc, (r + 1) * rc)
            x = x_ref[0, rs, :]
            hn_ref[rs, :] = _norm_mod(x, g_ref[...], sh_ref[0], sc_ref[0]).astype(BF16)
            o_ref[0, rs, :] = x

    hn = hn_ref[...]
    h1 = _dot(hn, w1_ref[...])
    h3 = _dot(hn, w3_ref[...])
    a = (_silu(h1) * h3).astype(BF16)
    scale = FFN_RES * gate_ref[0]
    tn = FFN_OUT_CHUNK
    for n in range(o_ref.shape[-1] // tn):
        cs = slice(n * tn, (n + 1) * tn)
        o_ref[0, :, cs] += scale[:, cs] * _dot(a, w2_ref[:, cs])


def _ffn(x, g, sh, sc, gate, w1, w3, w2, l, tm=1024, tf=512):
    B, S, D = x.shape
    F = w1.shape[2]
    nS = S // tm
    xmap = lambda i, j: (i // nS, i % nS, 0)
    bmap = lambda i, j: (i // nS, 0, 0)
    return pl.pallas_call(
        _ffn_kernel,
        grid=(B * nS, F // tf),
        in_specs=[pl.BlockSpec((1, tm, D), xmap),
                  pl.BlockSpec((1, D), lambda i, j: (0, 0)),
                  pl.BlockSpec((1, 1, D), bmap),
                  pl.BlockSpec((1, 1, D), bmap),
                  pl.BlockSpec((1, 1, D), bmap),
                  pl.BlockSpec((None, D, tf), lambda i, j: (l, 0, j)),
                  pl.BlockSpec((None, D, tf), lambda i, j: (l, 0, j)),
                  pl.BlockSpec((None, tf, D), lambda i, j: (l, j, 0))],
        out_specs=pl.BlockSpec((1, tm, D), xmap),
        out_shape=jax.ShapeDtypeStruct((B, S, D), F32),
        scratch_shapes=[pltpu.VMEM((tm, D), BF16)],
        compiler_params=_cparams(("arbitrary", "arbitrary")),
        name="ffn",
    )(x, g.reshape(1, D), sh.reshape(B, 1, D), sc.reshape(B, 1, D), gate.reshape(B, 1, D),
      w1, w3, w2)


def _proj_kernel(x_ref, g_ref, sh_ref, sc_ref, w_ref, o_ref, hn_ref):
    @pl.when(pl.program_id(1) == 0)
    def _():
        rc = FFN_NORM_ROWS
        for r in range(x_ref.shape[1] // rc):
            rs = slice(r * rc, (r + 1) * rc)
            hn_ref[rs, :] = _norm_mod(x_ref[0, rs, :], g_ref[...], sh_ref[0], sc_ref[0]).astype(BF16)

    o_ref[0] = _dot(hn_ref[...], w_ref[...])


def _proj(x, g, sh, sc, w, l, tm=1024, tn=PROJ_COL_TILE):
    B, S, D = x.shape
    N = w.shape[2]
    nS = S // tm
    xmap = lambda i, j: (i // nS, i % nS, 0)
    bmap = lambda i, j: (i // nS, 0, 0)
    return pl.pallas_call(
        _proj_kernel,
        grid=(B * nS, N // tn),
        in_specs=[pl.BlockSpec((1, tm, D), xmap, pipeline_mode=pl.Buffered(1)),
                  pl.BlockSpec((1, D), lambda i, j: (0, 0)),
                  pl.BlockSpec((1, 1, D), bmap),
                  pl.BlockSpec((1, 1, D), bmap),
                  pl.BlockSpec((None, D, tn), lambda i, j: (l, 0, j))],
        out_specs=pl.BlockSpec((1, tm, tn), lambda i, j: (i // nS, i % nS, j)),
        out_shape=jax.ShapeDtypeStruct((B, S, N), F32),
        scratch_shapes=[pltpu.VMEM((tm, D), BF16)],
        compiler_params=_cparams(("arbitrary", "arbitrary")),
        name="in_proj",
    )(x, g.reshape(1, D), sh.reshape(B, 1, D), sc.reshape(B, 1, D), w)


def _gdn_kernel(q_ref, k_ref, v_ref, z_ref, ba_ref, cq_ref, ck_ref, cv_ref, al_ref, dtb_ref,
                on_ref, o_ref, qn_s, kn_s, vv_s, qo_s, kp_s, cols_s, gct_s,
                *, S, HG):
    C = GDN_CHUNK
    HD = GDN_HEAD_DIM
    hg = pl.program_id(1)
    NH = GDN_HEADS
    shift = (LANES - hg * HG) % LANES
    t16 = pltpu.roll(ba_ref[0], shift, 1).T[0:2 * NH, :]
    a_col = al_ref[0][:, 0:1]
    d_col = dtb_ref[0][:, 0:1]
    beta16 = jax.nn.sigmoid(t16)
    xg = t16 + d_col
    g16 = -jnp.exp(a_col) * (jnp.maximum(xg, 0.0) + jnp.log1p(jnp.exp(-jnp.abs(xg))))
    lanec = lax.broadcasted_iota(jnp.int32, (2 * NH, S), 1) % C
    for d in (1, 2, 4, 8, 16, 32):
        g16 = g16 + jnp.where(lanec >= d, pltpu.roll(g16, d, 1), 0.0)
    gl16 = jnp.where(lanec == C - 1, g16, 0.0)
    for d in (1, 2, 4, 8, 16, 32):
        gl16 = gl16 + jnp.where(lanec + d < C, pltpu.roll(gl16, S - d, 1), 0.0)
    gct_s[...] = g16[NH:2 * NH, :]
    cols_s[...] = jnp.concatenate(
        [beta16[0:NH], g16[NH:2 * NH], gl16[NH:2 * NH], jnp.zeros((LANES - 3 * NH, S), F32)], axis=0).T

    SUB = 8
    row8 = lax.broadcasted_iota(jnp.int32, (SUB, HD), 0)

    def conv_silu(x_ref, w_ref, sl):
        w = w_ref[:, sl]
        K = GDN_CONV
        x0 = x_ref[0, 0:SUB, sl]
        head = x0 * w[K - 1:K, :]
        body = x_ref[0, SUB:S, sl] * w[K - 1:K, :]
        for kk in range(K - 1):
            s = K - 1 - kk
            head = head + jnp.where(row8 >= s, pltpu.roll(x0, s, 0), 0.0) * w[kk:kk + 1, :]
            body = body + x_ref[0, SUB - s:S - s, sl] * w[kk:kk + 1, :]
        return _silu(jnp.concatenate([head, body], axis=0))

    def l2n(x):
        return x * lax.rsqrt(jnp.sum(x * x, axis=-1, keepdims=True) + EPS)

    for j in range(HG):
        sl = slice(j * HD, (j + 1) * HD)
        qn_s[:, sl] = l2n(conv_silu(q_ref, cq_ref, sl)) * (HD ** -0.5)
        kn_s[:, sl] = l2n(conv_silu(k_ref, ck_ref, sl))
        vv_s[:, sl] = conv_silu(v_ref, cv_ref, sl)

    ii = lax.broadcasted_iota(jnp.int32, (C, C), 0)
    jj = lax.broadcasted_iota(jnp.int32, (C, C), 1)
    tril = ii >= jj
    strict = ii > jj
    eye = jnp.where(ii == jj, 1.0, 0.0)
    bd = {w: (ii // w) == (jj // w) for w in (8, 16, 32)}
    base_mask = strict & bd[8]
    merge_masks = (bd[16] & ~bd[8], bd[32] & ~bd[16], ~bd[32])
    NC = GDN_PHASE1_CHUNKS

    def phase1(p, carry):
        r0 = pl.multiple_of(p * NC * C, NC * C)
        gct = gct_s[:, pl.ds(r0, NC * C)]
        chains = []
        for cc in range(NC):
            rows = pl.ds(r0 + cc * C, C)
            cols = cols_s[rows, :]
            for j in range(HG):
                sl = slice(j * HD, (j + 1) * HD)
                gcol = cols[:, NH + j:NH + j + 1]
                glcol = cols[:, 2 * NH + j:2 * NH + j + 1]
                grow = gct[j:j + 1, cc * C:(cc + 1) * C]
                bcol = cols[:, j:j + 1]
                kn_c = kn_s[rows, sl]
                kb = kn_c * bcol
                eg = jnp.exp(gcol)
                chains.append(dict(
                    cc=cc, j=j, rows=rows, kn=kn_c, kb=kb, qn=qn_s[rows, sl], eg=eg,
                    egl=jnp.exp(glcol[0:1, :]),
                    decay=jnp.where(tril, jnp.exp(jnp.minimum(gcol - grow, 0.0)), 0.0),
                    rhs=jnp.concatenate([vv_s[rows, sl] * bcol, kb * eg], axis=1),
                    kdec=kn_c * jnp.exp(glcol - gcol)))
        for ch in chains:
            ch['lm'] = jnp.where(strict, _dot_nt(ch['kb'], ch['kn']) * ch['decay'], 0.0)
        for ch in chains:
            ch['attn'] = _dot_nt(ch['qn'], ch['kn']) * ch['decay']
        for ch in chains:
            ch['l8'] = jnp.where(base_mask, ch['lm'], 0.0)
            ch['s2'] = _dot(ch['l8'], ch['l8'])
        for ch in chains:
            n1 = eye - ch['l8']
            ch['n2'] = n1 + _dot(n1, ch['s2'])
        for ch in chains:
            ch['s4'] = _dot(ch['s2'], ch['s2'])
        for ch in chains:
            ch['t'] = ch['n2'] + _dot(ch['n2'], ch['s4'])
        for mm in merge_masks:
            for ch in chains:
                ch['tl'] = _dot(ch['t'], jnp.where(mm, ch['lm'], 0.0))
            for ch in chains:
                ch['t'] = ch['t'] - _dot(ch['tl'], ch['t'])
        for ch in chains:
            ch['x'] = _dot(ch['t'], ch['rhs'])
        for ch in chains:
            au = _dot(ch['attn'], ch['x'])
            ch['oq'] = (au[:, :HD], ch['qn'] * ch['eg'] - au[:, HD:])
        for ch in chains:
            ch['ku'] = _dot_tn(ch['kdec'], ch['x'])
        pairs = [(chains[cp * HG + j], chains[(cp + 1) * HG + j])
                 for cp in range(0, NC, 2) for j in range(HG)]
        for ch0, ch1 in pairs:
            j = ch0['j']
            o1, q1 = ch1['oq']
            pr = _dot(jnp.concatenate([ch1['ku'][:, HD:], q1], axis=0), ch0['ku'])
            e0, e1 = ch0['egl'], ch1['egl']
            b2 = e1 * ch0['ku'][:, :HD] - pr[:HD, :HD] + ch1['ku'][:, :HD]
            m2 = e1 * ch0['ku'][:, HD:] + e0 * ch1['ku'][:, HD:] - pr[:HD, HD:]
            kp_s[p * (NC // 2) + ch0['cc'] // 2, j] = jnp.concatenate([b2, m2], axis=1)
            cs = slice(j * 2 * HD, (j + 1) * 2 * HD)
            qo_s[ch0['rows'], cs] = jnp.concatenate(ch0['oq'], axis=1)
            qo_s[ch1['rows'], cs] = jnp.concatenate([pr[HD:, :HD] + o1, e0 * q1 - pr[HD:, HD:]], axis=1)
        return carry

    lax.fori_loop(0, S // (NC * C), phase1, 0)

    def gated_out(o, rows, sl):
        on = o * lax.rsqrt(jnp.mean(o * o, axis=-1, keepdims=True) + EPS) * on_ref[...]
        zc = z_ref[0, rows, sl]
        o_ref[0, rows, sl] = (on * _silu(zc)).astype(o_ref.dtype)

    def phase2(pi, hs):
        r0 = pl.multiple_of(pi * 2 * C, 2 * C)
        rows2 = pl.ds(r0, 2 * C)
        gla = cols_s[pl.ds(r0, SUB), :]
        glb = cols_s[pl.ds(r0 + C, SUB), :]
        st = []
        for j in range(HG):
            kp = kp_s[pi, j]
            qo = qo_s[rows2, j * 2 * HD:(j + 1) * 2 * HD]
            st.append((kp, qo, _dot(jnp.concatenate([kp[:, HD:], qo[:, HD:]], axis=0), hs[j])))
        new_hs = []
        for j in range(HG):
            kp, qo, r = st[j]
            e01 = jnp.exp(gla[0:1, 2 * NH + j:2 * NH + j + 1] + glb[0:1, 2 * NH + j:2 * NH + j + 1])
            new_hs.append(hs[j] * e01 - r[:HD] + kp[:, :HD])
            gated_out(r[HD:] + qo[:, :HD], rows2, slice(j * HD, (j + 1) * HD))
        return tuple(new_hs)

    h0 = tuple(jnp.zeros((HD, HD), F32) for _ in range(HG))
    lax.fori_loop(0, S // (2 * C), phase2, h0)


def _gdn(proj, conv_w, a_log, dt_bias, out_norm, off_aq, off_ba, HG=2):
    B, S, _ = proj.shape
    W = HG * GDN_HEAD_DIM
    nG = GDN_HEADS // HG

    def pad(v):
        rows = jnp.zeros((nG, 2 * GDN_HEADS), F32).at[:, GDN_HEADS:GDN_HEADS + HG].set(v.reshape(nG, HG))
        return jnp.broadcast_to(rows[:, :, None], (nG, 2 * GDN_HEADS, LANES))

    kern = functools.partial(_gdn_kernel, S=S, HG=HG)
    par = pl.BlockSpec((1, 2 * GDN_HEADS, LANES), lambda b, h: (h, 0, 0))
    col = lambda base: (lambda b, h: (b, 0, base // W + h))
    cw = lambda base: (lambda b, h: (0, base // W + h))
    one = lambda b, h: (0, 0)
    return pl.pallas_call(
        kern,
        grid=(B, nG),
        in_specs=[pl.BlockSpec((1, S, W), col(off_aq)),
                  pl.BlockSpec((1, S, W), col(off_aq + GDN_WIDTH)),
                  pl.BlockSpec((1, S, W), col(off_aq + 2 * GDN_WIDTH)),
                  pl.BlockSpec((1, S, W), col(OFF_Z)),
                  pl.BlockSpec((1, S, LANES), lambda b, h: (b, 0, off_ba // LANES)),
                  pl.BlockSpec((GDN_CONV, W), cw(0)),
                  pl.BlockSpec((GDN_CONV, W), cw(GDN_WIDTH)),
                  pl.BlockSpec((GDN_CONV, W), cw(2 * GDN_WIDTH)),
                  par, par,
                  pl.BlockSpec((1, GDN_HEAD_DIM), one)],
        out_specs=pl.BlockSpec((1, S, W), lambda b, h: (b, 0, h)),
        out_shape=jax.ShapeDtypeStruct((B, S, GDN_WIDTH), BF16),
        scratch_shapes=[pltpu.VMEM((S, W), F32), pltpu.VMEM((S, W), F32), pltpu.VMEM((S, W), F32),
                        pltpu.VMEM((S, 2 * W), F32),
                        pltpu.VMEM((S // (2 * GDN_CHUNK), HG, GDN_HEAD_DIM, 2 * GDN_HEAD_DIM), F32),
                        pltpu.VMEM((S, LANES), F32), pltpu.VMEM((GDN_HEADS, S), F32)],
        compiler_params=_cparams(("arbitrary", "arbitrary")),
        name="gdn",
    )(proj, proj, proj, proj, proj, conv_w, conv_w, conv_w, pad(a_log), pad(dt_bias),
      out_norm.reshape(1, GDN_HEAD_DIM))


def _s5_disc_kernel(are_ref, aim_ref, ls_ref, bre_ref, bim_ref, lr_ref, li_ref, br_ref, bi_ref):
    lam_re = jnp.minimum(are_ref[...], S5_MAX_RE)
    lam_im = aim_ref[...]
    step = jnp.exp(ls_ref[...])
    mag = jnp.exp(lam_re * step)
    lbr = mag * jnp.cos(lam_im * step)
    lbi = mag * jnp.sin(lam_im * step)
    lr_ref[...] = lbr
    li_ref[...] = lbi
    nr = lbr - 1.0
    den = lam_re * lam_re + lam_im * lam_im
    cr = (nr * lam_re + lbi * lam_im) / den
    ci = (lbi * lam_re - nr * lam_im) / den
    bre = bre_ref[...]
    bim = bim_ref[...]
    br_ref[...] = cr * bre - ci * bim
    bi_ref[...] = cr * bim + ci * bre


def _s5_discretize(a_re, a_im, log_step, b_re, b_im):
    G, P = a_re.shape
    I = b_re.shape[-1]
    gp = jax.ShapeDtypeStruct((G, 1, P), F32)
    gip = jax.ShapeDtypeStruct((G, I, P), F32)
    return pl.pallas_call(
        _s5_disc_kernel, out_shape=(gp, gp, gip, gip), name="s5_discretize",
    )(a_re.reshape(G, 1, P), a_im.reshape(G, 1, P), log_step.reshape(G, 1, 1),
      jnp.swapaxes(b_re, 1, 2), jnp.swapaxes(b_im, 1, 2))


def _s5_kernel(u_ref, pf_ref, pb_ref, bb_ref, lr_ref, li_ref, cc_ref, d_ref, gw_ref, gb_ref,
               o_ref, xr_s, xi_s, cr_s, ci_s, y_s, *, B, tt):
    TS = S5_TILE_STATES
    R = B * tt

    @pl.when(pl.program_id(0) == 0)
    def _():
        cr_s[...] = jnp.zeros_like(cr_s)
        ci_s[...] = jnp.zeros_like(ci_s)

    u = u_ref[...].reshape(R, S5_WIDTH)
    u_hi = u.astype(BF16)
    u_lo = (u - u_hi.astype(F32)).astype(BF16)
    up = _dot(pf_ref[...], jnp.concatenate([u_hi, u_lo], axis=1))
    u_tb = up[:, :S5_WIDTH] + up[:, S5_WIDTH:]
    for t in range(S5_TILES):
        ut = up[:, t * LANES:(t + 1) * LANES].astype(BF16)
        bu = _dot(ut, bb_ref[t])
        xr_s[:, :, t * TS:(t + 1) * TS] = bu[:, :TS].reshape(tt, B, TS)
        xi_s[:, :, t * TS:(t + 1) * TS] = bu[:, TS:].reshape(tt, B, TS)

    for t in range(S5_TILES):
        ls = slice(t * TS, (t + 1) * TS)
        ar = lr_ref[:, ls]
        ai = li_ref[:, ls]

        def step(s, carry, ls=ls, ar=ar, ai=ai):
            xr, xi = carry
            nr = ar * xr - ai * xi + xr_s[s, :, ls]
            ni = ar * xi + ai * xr + xi_s[s, :, ls]
            xr_s[s, :, ls] = nr
            xi_s[s, :, ls] = ni
            return nr, ni

        xr, xi = lax.fori_loop(0, tt, step, (cr_s[:, ls], ci_s[:, ls]), unroll=S5_SCAN_UNROLL)
        cr_s[:, ls] = xr
        ci_s[:, ls] = xi

    for t in range(S5_TILES):
        ls = slice(t * TS, (t + 1) * TS)
        lo = slice(t * LANES, (t + 1) * LANES)
        x = jnp.concatenate([xr_s[:, :, ls].reshape(R, TS), xi_s[:, :, ls].reshape(R, TS)], axis=1)
        y = _dot(x, cc_ref[t]) + d_ref[:, lo] * u_tb[:, lo]
        y_s[:, lo] = jax.nn.gelu(y).astype(BF16)
    z = _dot(y_s[...], gw_ref[...]) + gb_ref[...]
    out = (z[:, :S5_WIDTH] * jax.nn.sigmoid(z[:, S5_WIDTH:])).astype(BF16)
    o_ref[...] = _dot(pb_ref[...], out).reshape(B, tt, S5_WIDTH).astype(o_ref.dtype)


def _blockdiag_in(m):
    t = m.reshape(S5_TILES, S5_TILE_GROUPS, S5_GROUP, S5_STATE)
    eye = jnp.eye(S5_TILE_GROUPS, dtype=m.dtype)
    return jnp.einsum('tgip,gh->tgihp', t, eye).reshape(S5_TILES, LANES, S5_TILE_STATES)


def _blockdiag_out(m):
    t = m.reshape(S5_TILES, S5_TILE_GROUPS, S5_GROUP, S5_STATE)
    eye = jnp.eye(S5_TILE_GROUPS, dtype=m.dtype)
    return jnp.einsum('tgip,gh->tgphi', t, eye).reshape(S5_TILES, S5_TILE_STATES, LANES)


def _s5(proj, a_re, a_im, b_re, b_im, c_re, c_im, d_skip, log_step, glu_w, glu_b, tt=32):
    B, S, _ = proj.shape
    lr, li, bbr, bbi = _s5_discretize(a_re, a_im, log_step, b_re, b_im)
    NS = S5_GROUPS * S5_STATE
    kern = functools.partial(_s5_kernel, B=B, tt=tt)
    R = B * tt
    src = (jnp.arange(R) % B) * tt + jnp.arange(R) // B
    pf = (src[:, None] == jnp.arange(R)[None, :]).astype(BF16)
    bb = jnp.concatenate([_blockdiag_in(bbr), _blockdiag_in(bbi)], axis=2).astype(BF16)
    cc = jnp.concatenate([_blockdiag_out(c_re), -_blockdiag_out(c_im)], axis=1).astype(BF16)
    full2 = lambda i: (0, 0)
    full3 = lambda i: (0, 0, 0)
    return pl.pallas_call(
        kern,
        grid=(S // tt,),
        in_specs=[pl.BlockSpec((B, tt, S5_WIDTH), lambda i: (0, i, OFF_S5 // S5_WIDTH)),
                  pl.BlockSpec((R, R), full2),
                  pl.BlockSpec((R, R), full2),
                  pl.BlockSpec((S5_TILES, LANES, 2 * S5_TILE_STATES), full3),
                  pl.BlockSpec((1, NS), full2),
                  pl.BlockSpec((1, NS), full2),
                  pl.BlockSpec((S5_TILES, 2 * S5_TILE_STATES, LANES), full3),
                  pl.BlockSpec((1, S5_WIDTH), full2),
                  pl.BlockSpec((S5_WIDTH, 2 * S5_WIDTH), full2),
                  pl.BlockSpec((1, 2 * S5_WIDTH), full2)],
        out_specs=pl.BlockSpec((B, tt, S5_WIDTH), lambda i: (0, i, 0)),
        out_shape=jax.ShapeDtypeStruct((B, S, S5_WIDTH), BF16),
        scratch_shapes=[pltpu.VMEM((tt, B, NS), F32), pltpu.VMEM((tt, B, NS), F32),
                        pltpu.VMEM((B, NS), F32), pltpu.VMEM((B, NS), F32),
                        pltpu.VMEM((R, S5_WIDTH), BF16)],
        compiler_params=_cparams(("arbitrary",)),
        name="s5",
    )(proj, pf, pf.T, bb, lr.reshape(1, NS), li.reshape(1, NS), cc,
      d_skip.reshape(1, S5_WIDTH), glu_w.astype(BF16), glu_b.reshape(1, 2 * S5_WIDTH))


def _dil_kernel(q0, q1, q2, k0, k1, k2, v0, v1, v2, qg_ref, kg_ref, sl_ref, o_ref,
                qs, ks, vs, og, mg, lg, oacc, macc, lacc, *, S):
    q_refs, k_refs, v_refs = (q0, q1, q2), (k0, k1, k2), (v0, v1, v2)
    BLK = DIL_BLK
    HD = DIL_HEAD_DIM
    lane = lax.broadcasted_iota(jnp.int32, (1, LANES), 1)
    first = lane < HD
    ii = lax.broadcasted_iota(jnp.int32, (BLK, BLK), 0)
    jj = lax.broadcasted_iota(jnp.int32, (BLK, BLK), 1)
    steps_cur = (ii - jj).astype(F32)
    steps_prev = (ii - jj + BLK).astype(F32)
    ok_cur = ii >= jj
    ok_prev = jj >= ii

    li = lax.broadcasted_iota(jnp.int32, (LANES, LANES), 0) // HD
    lj = lax.broadcasted_iota(jnp.int32, (LANES, LANES), 1) // HD
    avg = jnp.where(li == lj, 1.0 / HD, 0.0).astype(BF16)

    def rms_pair(x, gain):
        x2 = x * x
        hi = x2.astype(BF16)
        lo = (x2 - hi.astype(F32)).astype(BF16)
        ms = _dot(hi, avg) + _dot(lo, avg)
        return x * lax.rsqrt(ms + EPS) * gain

    for g, (window, d) in enumerate(DIL_PAIRS):
        sd = S // d
        nb = sd // BLK
        for r in range(d):
            src = pl.ds(r, sd, stride=d) if d > 1 else slice(None)
            dst = slice(r * sd, (r + 1) * sd)
            qs[dst, :] = rms_pair(q_refs[g][0, src, :], qg_ref[...]) * (HD ** -0.5)
            ks[dst, :] = rms_pair(k_refs[g][0, src, :], kg_ref[...])
            vs[dst, :] = v_refs[g][0, src, :]

        sl_a = sl_ref[0, g:g + 1, 0:1] * float(d)
        sl_b = sl_ref[0, g:g + 1, HD:HD + 1] * float(d)
        bias = [(jnp.where(ok_cur, -s * steps_cur, NEG), jnp.where(ok_prev, -s * steps_prev, NEG))
                for s in (sl_a, sl_b)]

        def block_pair(ip, carry, nb=nb, bias=bias):
            st0 = pl.multiple_of(ip * 2 * BLK, 2 * BLK)
            ch = []
            for bi in range(2):
                rows = pl.ds(st0 + bi * BLK, BLK)
                blk = dict(rows=rows, q=qs[rows, :], kc=ks[rows, :], vc=vs[rows, :])
                if nb > 1 and bi == 0:
                    stp = pl.multiple_of(jnp.maximum(st0 - BLK, 0), BLK)
                    blk.update(kp=ks[pl.ds(stp, BLK), :], vp=vs[pl.ds(stp, BLK), :],
                               pen=jnp.where(((2 * ip) % nb) > 0, 0.0, NEG))
                elif nb > 1:
                    blk.update(kp=ch[0]['blk']['kc'], vp=ch[0]['blk']['vc'], pen=0.0)
                for hh in range(2):
                    qh = jnp.where(first, blk['q'], 0.0) if hh == 0 else jnp.where(first, 0.0, blk['q'])
                    ch.append(dict(blk=blk, hh=hh, qh=qh))
            for c in ch:
                c['sc'] = _dot_nt(c['qh'], c['blk']['kc']) + bias[c['hh']][0]
            if nb > 1:
                for c in ch:
                    c['sp'] = _dot_nt(c['qh'], c['blk']['kp']) + (bias[c['hh']][1] + c['blk']['pen'])
            for c in ch:
                m = jnp.max(c['sc'], axis=-1, keepdims=True)
                if nb > 1:
                    m = jnp.maximum(m, jnp.max(c['sp'], axis=-1, keepdims=True))
                c['m'] = m
                c['pc'] = jnp.exp(c['sc'] - m)
                c['l'] = jnp.sum(c['pc'], axis=-1, keepdims=True)
                if nb > 1:
                    c['pp'] = jnp.exp(c['sp'] - m)
                    c['l'] = c['l'] + jnp.sum(c['pp'], axis=-1, keepdims=True)
            for c in ch:
                c['o'] = _dot(c['pc'], c['blk']['vc'])
            if nb > 1:
                for c in ch:
                    c['o'] = c['o'] + _dot(c['pp'], c['blk']['vp'])
            for bi in range(2):
                a, b = ch[2 * bi], ch[2 * bi + 1]
                rows = a['blk']['rows']
                og[rows, :] = jnp.where(first, a['o'], b['o'])
                mg[rows, :] = jnp.where(first, a['m'], b['m'])
                lg[rows, :] = jnp.where(first, a['l'], b['l'])
            return carry

        lax.fori_loop(0, S // (2 * BLK), block_pair, 0)

        if g == 0:
            oacc[...] = og[...]
            macc[...] = mg[...]
            lacc[...] = lg[...]
        else:
            for r in range(d):
                tok = pl.ds(r, sd, stride=d)
                res = slice(r * sd, (r + 1) * sd)
                m_old = macc[tok, :]
                m_g = mg[res, :]
                m_new = jnp.maximum(m_old, m_g)
                a = jnp.exp(m_old - m_new)
                b = jnp.exp(m_g - m_new)
                lacc[tok, :] = a * lacc[tok, :] + b * lg[res, :]
                oacc[tok, :] = a * oacc[tok, :] + b * og[res, :]
                macc[tok, :] = m_new

    o_ref[0] = (oacc[...] / lacc[...]).astype(o_ref.dtype)


def _dil(proj, q_norm, k_norm):
    B, S, _ = proj.shape
    sub = jnp.arange(1, DIL_SUBHEADS + 1, dtype=F32)
    slopes = jnp.power(2.0, -ALIBI_MAX * sub / DIL_SUBHEADS)
    sl = slopes.reshape(len(DIL_PAIRS), 2, 2).transpose(1, 0, 2)
    sl = jnp.repeat(sl, DIL_HEAD_DIM, axis=2)
    tile2 = lambda v: jnp.tile(v.reshape(1, DIL_HEAD_DIM), (1, 2))
    kern = functools.partial(_dil_kernel, S=S)
    col = lambda base, g: (lambda b, hp: (b, 0, base // LANES + 2 * g + hp))
    specs = [pl.BlockSpec((1, S, LANES), col(base, g))
             for base in (OFF_CQ, OFF_CK, OFF_CV) for g in range(len(DIL_PAIRS))]
    one = lambda b, hp: (0, 0)
    seq = lambda: pltpu.VMEM((S, LANES), F32)
    return pl.pallas_call(
        kern,
        grid=(B, 2),
        in_specs=specs + [pl.BlockSpec((1, LANES), one), pl.BlockSpec((1, LANES), one),
                          pl.BlockSpec((1, len(DIL_PAIRS), LANES), lambda b, hp: (hp, 0, 0))],
        out_specs=pl.BlockSpec((1, S, LANES), lambda b, hp: (b, 0, hp)),
        out_shape=jax.ShapeDtypeStruct((B, S, DIL_OUT), BF16),
        scratch_shapes=[seq() for _ in range(9)],
        compiler_params=_cparams(("arbitrary", "arbitrary")),
        name="dilated_attn",
    )(*([proj] * 9), tile2(q_norm), tile2(k_norm), sl)


def _merge_kernel(x_ref, ya_ref, yb_ref, yc_ref, g0_ref, g1_ref, g2_ref, gm_ref,
                  wa_ref, wb_ref, wc_ref, wo_ref, o_ref):
    merged = (jax.nn.sigmoid(g0_ref[0]) * _dot(ya_ref[0], wa_ref[...])
              + jax.nn.sigmoid(g1_ref[0]) * _dot(yb_ref[0], wb_ref[...])
              + jax.nn.sigmoid(g2_ref[0]) * _dot(yc_ref[0], wc_ref[...]))
    out = _dot(merged.astype(BF16), wo_ref[...])
    o_ref[0] = x_ref[0] + gm_ref[0] * out


def _merge(x, ya, yb, yc, proj, gmod, wa, wb, wc, wo, tm=256):
    B, S, D = x.shape
    nS = S // tm
    rmap = lambda i: (i // nS, i % nS, 0)
    gate = lambda k: (lambda i: (i // nS, i % nS, OFF_GATES // D + k))
    const = lambda i: (0, 0)
    wspec = lambda w: pl.BlockSpec(w.shape, const, pipeline_mode=pl.Buffered(1))
    return pl.pallas_call(
        _merge_kernel,
        grid=(B * nS,),
        in_specs=[pl.BlockSpec((1, tm, D), rmap),
                  pl.BlockSpec((1, tm, ya.shape[-1]), rmap),
                  pl.BlockSpec((1, tm, yb.shape[-1]), rmap),
                  pl.BlockSpec((1, tm, yc.shape[-1]), rmap),
                  pl.BlockSpec((1, tm, D), gate(0)),
                  pl.BlockSpec((1, tm, D), gate(1)),
                  pl.BlockSpec((1, tm, D), gate(2)),
                  pl.BlockSpec((1, 1, D), lambda i: (i // nS, 0, 0)),
                  wspec(wa), wspec(wb), wspec(wc), wspec(wo)],
        out_specs=pl.BlockSpec((1, tm, D), rmap),
        out_shape=jax.ShapeDtypeStruct((B, S, D), F32),
        compiler_params=_cparams(("arbitrary",)),
        name="merge_out",
    )(x, ya, yb, yc, proj, proj, proj, gmod.reshape(B, 1, D), wa, wb, wc, wo)


def _permute_w_in(w_in, D):
    o = 0
    aq = w_in[..., o:o + 3 * GDN_WIDTH]; o += 3 * GDN_WIDTH
    az = w_in[..., o:o + GDN_WIDTH]; o += GDN_WIDTH
    ba = w_in[..., o:o + 2 * GDN_HEADS]; o += 2 * GDN_HEADS
    bu = w_in[..., o:o + S5_WIDTH]; o += S5_WIDTH
    cq = w_in[..., o:o + 3 * DIL_WIDTH]; o += 3 * DIL_WIDTH
    gt = w_in[..., o:o + N_BRANCH * D]
    used = w_in.shape[-1]
    total = -(-used // PROJ_COL_TILE) * PROJ_COL_TILE
    pad = jnp.zeros(w_in.shape[:-1] + (total - used,), w_in.dtype)
    return jnp.concatenate([bu, cq, az, gt, aq, ba, pad], axis=-1).astype(BF16)


def kernel(x, c, ada_w, ada_b, norm_ffn1, ffn1_w1, ffn1_w3, ffn1_w2, norm_mix, w_in, gdn_conv, gdn_a_log, gdn_dt_bias, gdn_out_norm, s5_a_re, s5_a_im, s5_b_re, s5_b_im, s5_c_re, s5_c_im, s5_d, s5_log_step, s5_glu_w, s5_glu_b, dil_q_norm, dil_k_norm, w_branch_a, w_branch_b, w_branch_c, w_out, norm_ffn2, ffn2_w1, ffn2_w3, ffn2_w2):
    B, S, D = x.shape
    L = ada_w.shape[0]
    off_aq = OFF_GATES + N_BRANCH * D
    off_ba = off_aq + 3 * GDN_WIDTH
    mods = _ada_mod(c, ada_w, ada_b)
    f1 = [w.astype(BF16) for w in (ffn1_w1, ffn1_w3, ffn1_w2)]
    f2 = [w.astype(BF16) for w in (ffn2_w1, ffn2_w3, ffn2_w2)]
    w_proj = _permute_w_in(w_in, D)
    for l in range(L):
        sh1, sc1, g1, sh2, sc2, g2, sh3, sc3, g3 = [mods[l, :, i * D:(i + 1) * D] for i in range(N_MOD)]
        x = _ffn(x, norm_ffn1[l], sh1, sc1, g1, *f1, l)
        proj = _proj(x, norm_mix[l], sh2, sc2, w_proj, l)
        ya = _gdn(proj, gdn_conv[l], gdn_a_log[l], gdn_dt_bias[l], gdn_out_norm[l], off_aq, off_ba)
        yb = _s5(proj, s5_a_re[l], s5_a_im[l], s5_b_re[l], s5_b_im[l], s5_c_re[l], s5_c_im[l],
                 s5_d[l], s5_log_step[l], s5_glu_w[l], s5_glu_b[l])
        yc = _dil(proj, dil_q_norm[l], dil_k_norm[l])
        x = _merge(x, ya, yb, yc, proj, g2, w_branch_a[l].astype(BF16), w_branch_b[l].astype(BF16),
                   w_branch_c[l].astype(BF16), w_out[l].astype(BF16))
        x = _ffn(x, norm_ffn2[l], sh3, sc3, g3, *f2, l)
    return x
```

```python
import functools
import math

import jax
import jax.numpy as jnp
from jax import lax
from jax.experimental import pallas as pl
from jax.experimental.pallas import tpu as pltpu

F32 = jnp.float32
BF16 = jnp.bfloat16
EPS = 1e-6
NEG = -1e30

LANES = 128
V7X_VMEM_LIMIT_BYTES = 56 * 1024 * 1024

FFN_RES = 0.5
FFN_OUT_CHUNK = 512
FFN_NORM_ROWS = 256
PROJ_COL_TILE = 2304
N_MOD = 9
GDN_HEADS = 8
GDN_HEAD_DIM = 128
GDN_WIDTH = GDN_HEADS * GDN_HEAD_DIM
GDN_CONV = 4
GDN_CHUNK = 64
GDN_PHASE1_CHUNKS = 8
GDN_CONV_ROWS = 256
S5_GROUP = 16
S5_WIDTH = 768
S5_GROUPS = S5_WIDTH // S5_GROUP
S5_STATE = 64
S5_MAX_RE = -1e-4
S5_TILE_GROUPS = LANES // S5_GROUP
S5_TILES = S5_GROUPS // S5_TILE_GROUPS
S5_TILE_STATES = S5_TILE_GROUPS * S5_STATE
S5_SCAN_UNROLL = 4
DIL_PAIRS = ((128, 1), (512, 4), (2048, 16))
DIL_HEADS_PER_GROUP = 4
DIL_HEAD_DIM = 64
DIL_SUBHEADS = len(DIL_PAIRS) * DIL_HEADS_PER_GROUP
DIL_WIDTH = DIL_SUBHEADS * DIL_HEAD_DIM
DIL_OUT = DIL_HEADS_PER_GROUP * DIL_HEAD_DIM
DIL_BLK = 128
ALIBI_MAX = 8.0
N_BRANCH = 3

OFF_S5 = 0
OFF_CQ = OFF_S5 + S5_WIDTH
OFF_CK = OFF_CQ + DIL_WIDTH
OFF_CV = OFF_CK + DIL_WIDTH
OFF_Z = OFF_CV + DIL_WIDTH
OFF_GATES = OFF_Z + GDN_WIDTH
OFF_AQ = None


def _cparams(sem, vmem=V7X_VMEM_LIMIT_BYTES):
    return pltpu.CompilerParams(dimension_semantics=sem, vmem_limit_bytes=vmem)


def _mxu(x):
    return x if x.dtype == BF16 else x.astype(BF16)


def _dot(a, b):
    return jnp.dot(_mxu(a), _mxu(b), preferred_element_type=F32)


def _dot_nt(a, b):
    return lax.dot_general(_mxu(a), _mxu(b), (((1,), (1,)), ((), ())), preferred_element_type=F32)


def _dot_tn(a, b):
    return lax.dot_general(_mxu(a), _mxu(b), (((0,), (0,)), ((), ())), preferred_element_type=F32)


def _silu(x):
    hx = 0.5 * x
    return hx * jnp.tanh(hx) + hx


def _ada_kernel(c_ref, w_ref, b_ref, o_ref):
    c = c_ref[...]
    o_ref[0] = _dot(_silu(c), w_ref[0]) + b_ref[0]


def _ada_mod(c, ada_w, ada_b, tn=1024):
    L, D, N = ada_w.shape
    B = c.shape[0]
    return pl.pallas_call(
        _ada_kernel,
        grid=(L, N // tn),
        in_specs=[pl.BlockSpec((B, D), lambda l, j: (0, 0)),
                  pl.BlockSpec((1, D, tn), lambda l, j: (l, 0, j)),
                  pl.BlockSpec((1, 1, tn), lambda l, j: (l, 0, j))],
        out_specs=pl.BlockSpec((1, B, tn), lambda l, j: (l, 0, j)),
        out_shape=jax.ShapeDtypeStruct((L, B, N), F32),
        compiler_params=_cparams(("arbitrary", "arbitrary")),
        name="ada_mod",
    )(c, ada_w, ada_b.reshape(L, 1, N))


def _norm_mod(x, g, sh, sc):
    ms = jnp.mean(x * x, axis=-1, keepdims=True)
    return (x * lax.rsqrt(ms + EPS)) * (g * (1.0 + sc)) + sh


def _ffn_kernel(x_ref, g_ref, sh_ref, sc_ref, gate_ref, w1_ref, w3_ref, w2_ref, o_ref, hn_ref):
    j = pl.program_id(1)

    @pl.when(j == 0)
    def _():
        rc = FFN_NORM_ROWS
        for r in range(x_ref.shape[1] // rc):
            rs = slice(r * rc, (r + 1) * rc)
            x = x_ref[0, rs, :]
            hn_ref[rs, :] = _norm_mod(x, g_ref[...], sh_ref[0], sc_ref[0]).astype(BF16)
            o_ref[0, rs, :] = x

    hn = hn_ref[...]
    h1 = _dot(hn, w1_ref[...])
    h3 = _dot(hn, w3_ref[...])
    a = (_silu(h1) * h3).astype(BF16)
    scale = FFN_RES * gate_ref[0]
    tn = FFN_OUT_CHUNK
    for n in range(o_ref.shape[-1] // tn):
        cs = slice(n * tn, (n + 1) * tn)
        o_ref[0, :, cs] += scale[:, cs] * _dot(a, w2_ref[:, cs])


def _ffn(x, g, sh, sc, gate, w1, w3, w2, l, tm=1024, tf=512):
    B, S, D = x.shape
    F = w1.shape[2]
    nS = S // tm
    xmap = lambda i, j: (i // nS, i % nS, 0)
    bmap = lambda i, j: (i // nS, 0, 0)
    return pl.pallas_call(
        _ffn_kernel,
        grid=(B * nS, F // tf),
        in_specs=[pl.BlockSpec((1, tm, D), xmap),
                  pl.BlockSpec((1, D), lambda i, j: (0, 0)),
                  pl.BlockSpec((1, 1, D), bmap),
                  pl.BlockSpec((1, 1, D), bmap),
                  pl.BlockSpec((1, 1, D), bmap),
                  pl.BlockSpec((None, D, tf), lambda i, j: (l, 0, j)),
                  pl.BlockSpec((None, D, tf), lambda i, j: (l, 0, j)),
                  pl.BlockSpec((None, tf, D), lambda i, j: (l, j, 0))],
        out_specs=pl.BlockSpec((1, tm, D), xmap),
        out_shape=jax.ShapeDtypeStruct((B, S, D), F32),
        scratch_shapes=[pltpu.VMEM((tm, D), BF16)],
        compiler_params=_cparams(("arbitrary", "arbitrary")),
        name="ffn",
    )(x, g.reshape(1, D), sh.reshape(B, 1, D), sc.reshape(B, 1, D), gate.reshape(B, 1, D),
      w1, w3, w2)


def _proj_kernel(x_ref, g_ref, sh_ref, sc_ref, w_ref, o_ref, hn_ref):
    @pl.when(pl.program_id(1) == 0)
    def _():
        rc = FFN_NORM_ROWS
        for r in range(x_ref.shape[1] // rc):
            rs = slice(r * rc, (r + 1) * rc)
            hn_ref[rs, :] = _norm_mod(x_ref[0, rs, :], g_ref[...], sh_ref[0], sc_ref[0]).astype(BF16)

    o_ref[0] = _dot(hn_ref[...], w_ref[...])


def _proj(x, g, sh, sc, w, l, tm=1024, tn=PROJ_COL_TILE):
    B, S, D = x.shape
    N = w.shape[2]
    nS = S // tm
    xmap = lambda i, j: (i // nS, i % nS, 0)
    bmap = lambda i, j: (i // nS, 0, 0)
    return pl.pallas_call(
        _proj_kernel,
        grid=(B * nS, N // tn),
        in_specs=[pl.BlockSpec((1, tm, D), xmap, pipeline_mode=pl.Buffered(1)),
                  pl.BlockSpec((1, D), lambda i, j: (0, 0)),
                  pl.BlockSpec((1, 1, D), bmap),
                  pl.BlockSpec((1, 1, D), bmap),
                  pl.BlockSpec((None, D, tn), lambda i, j: (l, 0, j))],
        out_specs=pl.BlockSpec((1, tm, tn), lambda i, j: (i // nS, i % nS, j)),
        out_shape=jax.ShapeDtypeStruct((B, S, N), F32),
        scratch_shapes=[pltpu.VMEM((tm, D), BF16)],
        compiler_params=_cparams(("arbitrary", "arbitrary")),
        name="in_proj",
    )(x, g.reshape(1, D), sh.reshape(B, 1, D), sc.reshape(B, 1, D), w)


def _gdn_kernel(q_ref, k_ref, v_ref, z_ref, ba_ref, cq_ref, ck_ref, cv_ref, al_ref, dtb_ref,
                on_ref, o_ref, qn_s, kn_s, vv_s, qo_s, kp_s, cols_s, gct_s,
                *, S, HG):
    C = GDN_CHUNK
    HD = GDN_HEAD_DIM
    hg = pl.program_id(1)
    NH = GDN_HEADS
    shift = (LANES - hg * HG) % LANES
    t16 = pltpu.roll(ba_ref[0], shift, 1).T[0:2 * NH, :]
    a_col = al_ref[0][:, 0:1]
    d_col = dtb_ref[0][:, 0:1]
    beta16 = jax.nn.sigmoid(t16)
    xg = t16 + d_col
    g16 = -jnp.exp(a_col) * (jnp.maximum(xg, 0.0) + jnp.log1p(jnp.exp(-jnp.abs(xg))))
    lanec = lax.broadcasted_iota(jnp.int32, (2 * NH, S), 1) % C
    for d in (1, 2, 4, 8, 16, 32):
        g16 = g16 + jnp.where(lanec >= d, pltpu.roll(g16, d, 1), 0.0)
    gl16 = jnp.where(lanec == C - 1, g16, 0.0)
    for d in (1, 2, 4, 8, 16, 32):
        gl16 = gl16 + jnp.where(lanec + d < C, pltpu.roll(gl16, S - d, 1), 0.0)
    gct_s[...] = g16[NH:2 * NH, :]
    cols_s[...] = jnp.concatenate(
        [beta16[0:NH], g16[NH:2 * NH], gl16[NH:2 * NH], jnp.zeros((LANES - 3 * NH, S), F32)], axis=0).T

    SUB = 8
    RB = GDN_CONV_ROWS

    def l2n(x, scale=1.0):
        return x * (lax.rsqrt(jnp.sum(x * x, axis=-1, keepdims=True) + EPS) * scale)

    def conv_block(rb, carry):
        r0 = pl.multiple_of(rb * RB, RB)
        rp = pl.multiple_of(jnp.maximum(r0 - SUB, 0), SUB)
        keep = jnp.where(rb > 0, 1.0, 0.0)
        rows = pl.ds(r0, RB)
        K = GDN_CONV
        for j in range(HG):
            sl = slice(j * HD, (j + 1) * HD)
            for x_ref, w_ref, dst, post in ((q_ref, cq_ref, qn_s, lambda y: l2n(y, HD ** -0.5)),
                                            (k_ref, ck_ref, kn_s, l2n),
                                            (v_ref, cv_ref, vv_s, lambda y: y)):
                w = w_ref[:, sl]
                cur = x_ref[0, rows, sl]
                hist = jnp.concatenate([x_ref[0, pl.ds(rp, SUB), sl] * keep, cur], axis=0)
                acc = cur * w[K - 1:K, :]
                for kk in range(K - 1):
                    acc = acc + pltpu.roll(hist, K - 1 - kk, 0)[SUB:] * w[kk:kk + 1, :]
                dst[rows, sl] = post(_silu(acc))
        return carry

    lax.fori_loop(0, S // RB, conv_block, 0)

    ii = lax.broadcasted_iota(jnp.int32, (C, C), 0)
    jj = lax.broadcasted_iota(jnp.int32, (C, C), 1)
    tril = ii >= jj
    strict = ii > jj
    eye = jnp.where(ii == jj, 1.0, 0.0)
    bd = {w: (ii // w) == (jj // w) for w in (8, 16, 32)}
    base_mask = strict & bd[8]
    merge_masks = (bd[16] & ~bd[8], bd[32] & ~bd[16], ~bd[32])
    NC = GDN_PHASE1_CHUNKS

    def phase1(p, carry):
        r0 = pl.multiple_of(p * NC * C, NC * C)
        gct = gct_s[:, pl.ds(r0, NC * C)]
        chains = []
        for cc in range(NC):
            rows = pl.ds(r0 + cc * C, C)
            cols = cols_s[rows, :]
            for j in range(HG):
                sl = slice(j * HD, (j + 1) * HD)
                gcol = cols[:, NH + j:NH + j + 1]
                glcol = cols[:, 2 * NH + j:2 * NH + j + 1]
                grow = gct[j:j + 1, cc * C:(cc + 1) * C]
                bcol = cols[:, j:j + 1]
                kn_c = kn_s[rows, sl]
                kb = kn_c * bcol
                eg = jnp.exp(gcol)
                chains.append(dict(
                    cc=cc, j=j, rows=rows, kn=kn_c, kb=kb, qn=qn_s[rows, sl], eg=eg,
                    egl=jnp.exp(glcol[0:1, :]),
                    decay=jnp.where(tril, jnp.exp(jnp.minimum(gcol - grow, 0.0)), 0.0),
                    rhs=jnp.concatenate([vv_s[rows, sl] * bcol, kb * eg], axis=1),
                    kdec=kn_c * jnp.exp(glcol - gcol)))
        for ch in chains:
            ch['lm'] = jnp.where(strict, _dot_nt(ch['kb'], ch['kn']) * ch['decay'], 0.0)
        for ch in chains:
            ch['attn'] = _dot_nt(ch['qn'], ch['kn']) * ch['decay']
        for ch in chains:
            ch['l8'] = jnp.where(base_mask, ch['lm'], 0.0)
            ch['s2'] = _dot(ch['l8'], ch['l8'])
        for ch in chains:
            n1 = eye - ch['l8']
            ch['n2'] = n1 + _dot(n1, ch['s2'])
        for ch in chains:
            ch['s4'] = _dot(ch['s2'], ch['s2'])
        for ch in chains:
            ch['t'] = ch['n2'] + _dot(ch['n2'], ch['s4'])
        for mm in merge_masks:
            for ch in chains:
                ch['tl'] = _dot(ch['t'], jnp.where(mm, ch['lm'], 0.0))
            for ch in chains:
                ch['t'] = ch['t'] - _dot(ch['tl'], ch['t'])
        for ch in chains:
            ch['x'] = _dot(ch['t'], ch['rhs'])
        for ch in chains:
            au = _dot(ch['attn'], ch['x'])
            ch['oq'] = (au[:, :HD], ch['qn'] * ch['eg'] - au[:, HD:])
        for ch in chains:
            ch['ku'] = _dot_tn(ch['kdec'], ch['x'])
        pairs = [(chains[cp * HG + j], chains[(cp + 1) * HG + j])
                 for cp in range(0, NC, 2) for j in range(HG)]
        for ch0, ch1 in pairs:
            j = ch0['j']
            o1, q1 = ch1['oq']
            pr = _dot(jnp.concatenate([ch1['ku'][:, HD:], q1], axis=0), ch0['ku'])
            e0, e1 = ch0['egl'], ch1['egl']
            b2 = e1 * ch0['ku'][:, :HD] - pr[:HD, :HD] + ch1['ku'][:, :HD]
            m2 = e1 * ch0['ku'][:, HD:] + e0 * ch1['ku'][:, HD:] - pr[:HD, HD:]
            kp_s[p * (NC // 2) + ch0['cc'] // 2, j] = jnp.concatenate([b2, m2], axis=1)
            cs = slice(j * 2 * HD, (j + 1) * 2 * HD)
            qo_s[ch0['rows'], cs] = jnp.concatenate(ch0['oq'], axis=1)
            qo_s[ch1['rows'], cs] = jnp.concatenate([pr[HD:, :HD] + o1, e0 * q1 - pr[HD:, HD:]], axis=1)
        return carry

    lax.fori_loop(0, S // (NC * C), phase1, 0)

    def gated_out(o, rows, sl):
        on = o * lax.rsqrt(jnp.mean(o * o, axis=-1, keepdims=True) + EPS) * on_ref[...]
        zc = z_ref[0, rows, sl]
        o_ref[0, rows, sl] = (on * _silu(zc)).astype(o_ref.dtype)

    def phase2(pi, hs):
        r0 = pl.multiple_of(pi * 2 * C, 2 * C)
        rows2 = pl.ds(r0, 2 * C)
        gla = cols_s[pl.ds(r0, SUB), :]
        glb = cols_s[pl.ds(r0 + C, SUB), :]
        st = []
        for j in range(HG):
            kp = kp_s[pi, j]
            qo = qo_s[rows2, j * 2 * HD:(j + 1) * 2 * HD]
            st.append((kp, qo, _dot(jnp.concatenate([kp[:, HD:], qo[:, HD:]], axis=0), hs[j])))
        new_hs = []
        for j in range(HG):
            kp, qo, r = st[j]
            e01 = jnp.exp(gla[0:1, 2 * NH + j:2 * NH + j + 1] + glb[0:1, 2 * NH + j:2 * NH + j + 1])
            new_hs.append(hs[j] * e01 - r[:HD] + kp[:, :HD])
            gated_out(r[HD:] + qo[:, :HD], rows2, slice(j * HD, (j + 1) * HD))
        return tuple(new_hs)

    h0 = tuple(jnp.zeros((HD, HD), F32) for _ in range(HG))
    lax.fori_loop(0, S // (2 * C), phase2, h0)


def _gdn(proj, conv_w, a_log, dt_bias, out_norm, off_aq, off_ba, HG=2):
    B, S, _ = proj.shape
    W = HG * GDN_HEAD_DIM
    nG = GDN_HEADS // HG

    def pad(v):
        rows = jnp.zeros((nG, 2 * GDN_HEADS), F32).at[:, GDN_HEADS:GDN_HEADS + HG].set(v.reshape(nG, HG))
        return jnp.broadcast_to(rows[:, :, None], (nG, 2 * GDN_HEADS, LANES))

    kern = functools.partial(_gdn_kernel, S=S, HG=HG)
    par = pl.BlockSpec((1, 2 * GDN_HEADS, LANES), lambda b, h: (h, 0, 0))
    col = lambda base: (lambda b, h: (b, 0, base // W + h))
    cw = lambda base: (lambda b, h: (0, base // W + h))
    one = lambda b, h: (0, 0)
    return pl.pallas_call(
        kern,
        grid=(B, nG),
        in_specs=[pl.BlockSpec((1, S, W), col(off_aq)),
                  pl.BlockSpec((1, S, W), col(off_aq + GDN_WIDTH)),
                  pl.BlockSpec((1, S, W), col(off_aq + 2 * GDN_WIDTH)),
                  pl.BlockSpec((1, S, W), col(OFF_Z)),
                  pl.BlockSpec((1, S, LANES), lambda b, h: (b, 0, off_ba // LANES)),
                  pl.BlockSpec((GDN_CONV, W), cw(0)),
                  pl.BlockSpec((GDN_CONV, W), cw(GDN_WIDTH)),
                  pl.BlockSpec((GDN_CONV, W), cw(2 * GDN_WIDTH)),
                  par, par,
                  pl.BlockSpec((1, GDN_HEAD_DIM), one)],
        out_specs=pl.BlockSpec((1, S, W), lambda b, h: (b, 0, h)),
        out_shape=jax.ShapeDtypeStruct((B, S, GDN_WIDTH), BF16),
        scratch_shapes=[pltpu.VMEM((S, W), F32), pltpu.VMEM((S, W), F32), pltpu.VMEM((S, W), F32),
                        pltpu.VMEM((S, 2 * W), F32),
                        pltpu.VMEM((S // (2 * GDN_CHUNK), HG, GDN_HEAD_DIM, 2 * GDN_HEAD_DIM), F32),
                        pltpu.VMEM((S, LANES), F32), pltpu.VMEM((GDN_HEADS, S), F32)],
        compiler_params=_cparams(("arbitrary", "arbitrary")),
        name="gdn",
    )(proj, proj, proj, proj, proj, conv_w, conv_w, conv_w, pad(a_log), pad(dt_bias),
      out_norm.reshape(1, GDN_HEAD_DIM))


def _s5_disc_kernel(are_ref, aim_ref, ls_ref, bre_ref, bim_ref, lr_ref, li_ref, br_ref, bi_ref):
    lam_re = jnp.minimum(are_ref[...], S5_MAX_RE)
    lam_im = aim_ref[...]
    step = jnp.exp(ls_ref[...])
    mag = jnp.exp(lam_re * step)
    lbr = mag * jnp.cos(lam_im * step)
    lbi = mag * jnp.sin(lam_im * step)
    lr_ref[...] = lbr
    li_ref[...] = lbi
    nr = lbr - 1.0
    den = lam_re * lam_re + lam_im * lam_im
    cr = (nr * lam_re + lbi * lam_im) / den
    ci = (lbi * lam_re - nr * lam_im) / den
    bre = bre_ref[...]
    bim = bim_ref[...]
    br_ref[...] = cr * bre - ci * bim
    bi_ref[...] = cr * bim + ci * bre


def _s5_discretize(a_re, a_im, log_step, b_re, b_im):
    G, P = a_re.shape
    I = b_re.shape[-1]
    gp = jax.ShapeDtypeStruct((G, 1, P), F32)
    gip = jax.ShapeDtypeStruct((G, I, P), F32)
    return pl.pallas_call(
        _s5_disc_kernel, out_shape=(gp, gp, gip, gip), name="s5_discretize",
    )(a_re.reshape(G, 1, P), a_im.reshape(G, 1, P), log_step.reshape(G, 1, 1),
      jnp.swapaxes(b_re, 1, 2), jnp.swapaxes(b_im, 1, 2))


def _s5_kernel(u_ref, pf_ref, pb_ref, bb_ref, lr_ref, li_ref, cc_ref, d_ref, gw_ref, gb_ref,
               o_ref, xr_s, xi_s, cr_s, ci_s, y_s, *, B, tt):
    TS = S5_TILE_STATES
    R = B * tt

    @pl.when(pl.program_id(0) == 0)
    def _():
        cr_s[...] = jnp.zeros_like(cr_s)
        ci_s[...] = jnp.zeros_like(ci_s)

    u = u_ref[...].reshape(R, S5_WIDTH)
    u_hi = u.astype(BF16)
    u_lo = (u - u_hi.astype(F32)).astype(BF16)
    up = _dot(pf_ref[...], jnp.concatenate([u_hi, u_lo], axis=1))
    u_tb = up[:, :S5_WIDTH] + up[:, S5_WIDTH:]
    for t in range(S5_TILES):
        ut = up[:, t * LANES:(t + 1) * LANES].astype(BF16)
        bu = _dot(ut, bb_ref[t])
        xr_s[:, :, t * TS:(t + 1) * TS] = bu[:, :TS].reshape(tt, B, TS)
        xi_s[:, :, t * TS:(t + 1) * TS] = bu[:, TS:].reshape(tt, B, TS)

    for t in range(S5_TILES):
        ls = slice(t * TS, (t + 1) * TS)
        ar = lr_ref[:, ls]
        ai = li_ref[:, ls]

        def step(s, carry, ls=ls, ar=ar, ai=ai):
            xr, xi = carry
            nr = ar * xr - ai * xi + xr_s[s, :, ls]
            ni = ar * xi + ai * xr + xi_s[s, :, ls]
            xr_s[s, :, ls] = nr
            xi_s[s, :, ls] = ni
            return nr, ni

        xr, xi = lax.fori_loop(0, tt, step, (cr_s[:, ls], ci_s[:, ls]), unroll=S5_SCAN_UNROLL)
        cr_s[:, ls] = xr
        ci_s[:, ls] = xi

    for t in range(S5_TILES):
        ls = slice(t * TS, (t + 1) * TS)
        lo = slice(t * LANES, (t + 1) * LANES)
        x = jnp.concatenate([xr_s[:, :, ls].reshape(R, TS), xi_s[:, :, ls].reshape(R, TS)], axis=1)
        y = _dot(x, cc_ref[t]) + d_ref[:, lo] * u_tb[:, lo]
        y_s[:, lo] = jax.nn.gelu(y).astype(BF16)
    z = _dot(y_s[...], gw_ref[...]) + gb_ref[...]
    out = (z[:, :S5_WIDTH] * jax.nn.sigmoid(z[:, S5_WIDTH:])).astype(BF16)
    o_ref[...] = _dot(pb_ref[...], out).reshape(B, tt, S5_WIDTH).astype(o_ref.dtype)


def _blockdiag_in(m):
    t = m.reshape(S5_TILES, S5_TILE_GROUPS, S5_GROUP, S5_STATE)
    eye = jnp.eye(S5_TILE_GROUPS, dtype=m.dtype)
    return jnp.einsum('tgip,gh->tgihp', t, eye).reshape(S5_TILES, LANES, S5_TILE_STATES)


def _blockdiag_out(m):
    t = m.reshape(S5_TILES, S5_TILE_GROUPS, S5_GROUP, S5_STATE)
    eye = jnp.eye(S5_TILE_GROUPS, dtype=m.dtype)
    return jnp.einsum('tgip,gh->tgphi', t, eye).reshape(S5_TILES, S5_TILE_STATES, LANES)


def _s5(proj, a_re, a_im, b_re, b_im, c_re, c_im, d_skip, log_step, glu_w, glu_b, tt=32):
    B, S, _ = proj.shape
    lr, li, bbr, bbi = _s5_discretize(a_re, a_im, log_step, b_re, b_im)
    NS = S5_GROUPS * S5_STATE
    kern = functools.partial(_s5_kernel, B=B, tt=tt)
    R = B * tt
    src = (jnp.arange(R) % B) * tt + jnp.arange(R) // B
    pf = (src[:, None] == jnp.arange(R)[None, :]).astype(BF16)
    bb = jnp.concatenate([_blockdiag_in(bbr), _blockdiag_in(bbi)], axis=2).astype(BF16)
    cc = jnp.concatenate([_blockdiag_out(c_re), -_blockdiag_out(c_im)], axis=1).astype(BF16)
    full2 = lambda i: (0, 0)
    full3 = lambda i: (0, 0, 0)
    return pl.pallas_call(
        kern,
        grid=(S // tt,),
        in_specs=[pl.BlockSpec((B, tt, S5_WIDTH), lambda i: (0, i, OFF_S5 // S5_WIDTH)),
                  pl.BlockSpec((R, R), full2),
                  pl.BlockSpec((R, R), full2),
                  pl.BlockSpec((S5_TILES, LANES, 2 * S5_TILE_STATES), full3),
                  pl.BlockSpec((1, NS), full2),
                  pl.BlockSpec((1, NS), full2),
                  pl.BlockSpec((S5_TILES, 2 * S5_TILE_STATES, LANES), full3),
                  pl.BlockSpec((1, S5_WIDTH), full2),
                  pl.BlockSpec((S5_WIDTH, 2 * S5_WIDTH), full2),
                  pl.BlockSpec((1, 2 * S5_WIDTH), full2)],
        out_specs=pl.BlockSpec((B, tt, S5_WIDTH), lambda i: (0, i, 0)),
        out_shape=jax.ShapeDtypeStruct((B, S, S5_WIDTH), BF16),
        scratch_shapes=[pltpu.VMEM((tt, B, NS), F32), pltpu.VMEM((tt, B, NS), F32),
                        pltpu.VMEM((B, NS), F32), pltpu.VMEM((B, NS), F32),
                        pltpu.VMEM((R, S5_WIDTH), BF16)],
        compiler_params=_cparams(("arbitrary",)),
        name="s5",
    )(proj, pf, pf.T, bb, lr.reshape(1, NS), li.reshape(1, NS), cc,
      d_skip.reshape(1, S5_WIDTH), glu_w.astype(BF16), glu_b.reshape(1, 2 * S5_WIDTH))


def _dil_kernel(q0, q1, q2, k0, k1, k2, v0, v1, v2, qg_ref, kg_ref, sl_ref, o_ref,
                qs, ks, vs, og, mg, lg, oacc, macc, lacc, *, S):
    q_refs, k_refs, v_refs = (q0, q1, q2), (k0, k1, k2), (v0, v1, v2)
    BLK = DIL_BLK
    HD = DIL_HEAD_DIM
    lane = lax.broadcasted_iota(jnp.int32, (1, LANES), 1)
    first = lane < HD
    ii = lax.broadcasted_iota(jnp.int32, (BLK, BLK), 0)
    jj = lax.broadcasted_iota(jnp.int32, (BLK, BLK), 1)
    steps_cur = (ii - jj).astype(F32)
    steps_prev = (ii - jj + BLK).astype(F32)
    ok_cur = ii >= jj
    ok_prev = jj >= ii

    li = lax.broadcasted_iota(jnp.int32, (LANES, LANES), 0) // HD
    lj = lax.broadcasted_iota(jnp.int32, (LANES, LANES), 1) // HD
    avg = jnp.where(li == lj, 1.0 / HD, 0.0).astype(BF16)

    def rms_pair(x, gain):
        x2 = x * x
        hi = x2.astype(BF16)
        lo = (x2 - hi.astype(F32)).astype(BF16)
        ms = _dot(hi, avg) + _dot(lo, avg)
        return x * lax.rsqrt(ms + EPS) * gain

    for g, (window, d) in enumerate(DIL_PAIRS):
        sd = S // d
        nb = sd // BLK
        for r in range(d):
            src = pl.ds(r, sd, stride=d) if d > 1 else slice(None)
            dst = slice(r * sd, (r + 1) * sd)
            qs[dst, :] = rms_pair(q_refs[g][0, src, :], qg_ref[...]) * (HD ** -0.5)
            ks[dst, :] = rms_pair(k_refs[g][0, src, :], kg_ref[...])
            vs[dst, :] = v_refs[g][0, src, :]

        sl_a = sl_ref[0, g:g + 1, 0:1] * float(d)
        sl_b = sl_ref[0, g:g + 1, HD:HD + 1] * float(d)
        bias = [(jnp.where(ok_cur, -s * steps_cur, NEG), jnp.where(ok_prev, -s * steps_prev, NEG))
                for s in (sl_a, sl_b)]

        def block_pair(ip, carry, nb=nb, bias=bias):
            st0 = pl.multiple_of(ip * 2 * BLK, 2 * BLK)
            ch = []
            for bi in range(2):
                rows = pl.ds(st0 + bi * BLK, BLK)
                blk = dict(rows=rows, q=qs[rows, :], kc=ks[rows, :], vc=vs[rows, :])
                if nb > 1 and bi == 0:
                    stp = pl.multiple_of(jnp.maximum(st0 - BLK, 0), BLK)
                    blk.update(kp=ks[pl.ds(stp, BLK), :], vp=vs[pl.ds(stp, BLK), :],
                               pen=jnp.where(((2 * ip) % nb) > 0, 0.0, NEG))
                elif nb > 1:
                    blk.update(kp=ch[0]['blk']['kc'], vp=ch[0]['blk']['vc'], pen=0.0)
                for hh in range(2):
                    qh = jnp.where(first, blk['q'], 0.0) if hh == 0 else jnp.where(first, 0.0, blk['q'])
                    ch.append(dict(blk=blk, hh=hh, qh=qh))
            for c in ch:
                c['sc'] = _dot_nt(c['qh'], c['blk']['kc']) + bias[c['hh']][0]
            if nb > 1:
                for c in ch:
                    c['sp'] = _dot_nt(c['qh'], c['blk']['kp']) + (bias[c['hh']][1] + c['blk']['pen'])
            for c in ch:
                m = jnp.max(c['sc'], axis=-1, keepdims=True)
                if nb > 1:
                    m = jnp.maximum(m, jnp.max(c['sp'], axis=-1, keepdims=True))
                c['m'] = m
                c['pc'] = jnp.exp(c['sc'] - m)
                c['l'] = jnp.sum(c['pc'], axis=-1, keepdims=True)
                if nb > 1:
                    c['pp'] = jnp.exp(c['sp'] - m)
                    c['l'] = c['l'] + jnp.sum(c['pp'], axis=-1, keepdims=True)
            for c in ch:
                c['o'] = _dot(c['pc'], c['blk']['vc'])
            if nb > 1:
                for c in ch:
                    c['o'] = c['o'] + _dot(c['pp'], c['blk']['vp'])
            for bi in range(2):
                a, b = ch[2 * bi], ch[2 * bi + 1]
                rows = a['blk']['rows']
                og[rows, :] = jnp.where(first, a['o'], b['o'])
                mg[rows, :] = jnp.where(first, a['m'], b['m'])
                lg[rows, :] = jnp.where(first, a['l'], b['l'])
            return carry

        lax.fori_loop(0, S // (2 * BLK), block_pair, 0)

        if g == 0:
            oacc[...] = og[...]
            macc[...] = mg[...]
            lacc[...] = lg[...]
        else:
            for r in range(d):
                tok = pl.ds(r, sd, stride=d)
                res = slice(r * sd, (r + 1) * sd)
                m_old = macc[tok, :]
                m_g = mg[res, :]
                m_new = jnp.maximum(m_old, m_g)
                a = jnp.exp(m_old - m_new)
                b = jnp.exp(m_g - m_new)
                lacc[tok, :] = a * lacc[tok, :] + b * lg[res, :]
                oacc[tok, :] = a * oacc[tok, :] + b * og[res, :]
                macc[tok, :] = m_new

    o_ref[0] = (oacc[...] / lacc[...]).astype(o_ref.dtype)


def _dil(proj, q_norm, k_norm):
    B, S, _ = proj.shape
    sub = jnp.arange(1, DIL_SUBHEADS + 1, dtype=F32)
    slopes = jnp.power(2.0, -ALIBI_MAX * sub / DIL_SUBHEADS)
    sl = slopes.reshape(len(DIL_PAIRS), 2, 2).transpose(1, 0, 2)
    sl = jnp.repeat(sl, DIL_HEAD_DIM, axis=2)
    tile2 = lambda v: jnp.tile(v.reshape(1, DIL_HEAD_DIM), (1, 2))
    kern = functools.partial(_dil_kernel, S=S)
    col = lambda base, g: (lambda b, hp: (b, 0, base // LANES + 2 * g + hp))
    specs = [pl.BlockSpec((1, S, LANES), col(base, g))
             for base in (OFF_CQ, OFF_CK, OFF_CV) for g in range(len(DIL_PAIRS))]
    one = lambda b, hp: (0, 0)
    seq = lambda: pltpu.VMEM((S, LANES), F32)
    return pl.pallas_call(
        kern,
        grid=(B, 2),
        in_specs=specs + [pl.BlockSpec((1, LANES), one), pl.BlockSpec((1, LANES), one),
                          pl.BlockSpec((1, len(DIL_PAIRS), LANES), lambda b, hp: (hp, 0, 0))],
        out_specs=pl.BlockSpec((1, S, LANES), lambda b, hp: (b, 0, hp)),
        out_shape=jax.ShapeDtypeStruct((B, S, DIL_OUT), BF16),
        scratch_shapes=[seq() for _ in range(9)],
        compiler_params=_cparams(("arbitrary", "arbitrary")),
        name="dilated_attn",
    )(*([proj] * 9), tile2(q_norm), tile2(k_norm), sl)


def _merge_kernel(x_ref, ya_ref, yb_ref, yc_ref, g0_ref, g1_ref, g2_ref, gm_ref,
                  wa_ref, wb_ref, wc_ref, wo_ref, o_ref):
    merged = (jax.nn.sigmoid(g0_ref[0]) * _dot(ya_ref[0], wa_ref[...])
              + jax.nn.sigmoid(g1_ref[0]) * _dot(yb_ref[0], wb_ref[...])
              + jax.nn.sigmoid(g2_ref[0]) * _dot(yc_ref[0], wc_ref[...]))
    out = _dot(merged.astype(BF16), wo_ref[...])
    o_ref[0] = x_ref[0] + gm_ref[0] * out


def _merge(x, ya, yb, yc, proj, gmod, wa, wb, wc, wo, tm=256):
    B, S, D = x.shape
    nS = S // tm
    rmap = lambda i: (i // nS, i % nS, 0)
    gate = lambda k: (lambda i: (i // nS, i % nS, OFF_GATES // D + k))
    const = lambda i: (0, 0)
    wspec = lambda w: pl.BlockSpec(w.shape, const, pipeline_mode=pl.Buffered(1))
    return pl.pallas_call(
        _merge_kernel,
        grid=(B * nS,),
        in_specs=[pl.BlockSpec((1, tm, D), rmap),
                  pl.BlockSpec((1, tm, ya.shape[-1]), rmap),
                  pl.BlockSpec((1, tm, yb.shape[-1]), rmap),
                  pl.BlockSpec((1, tm, yc.shape[-1]), rmap),
                  pl.BlockSpec((1, tm, D), gate(0)),
                  pl.BlockSpec((1, tm, D), gate(1)),
                  pl.BlockSpec((1, tm, D), gate(2)),
                  pl.BlockSpec((1, 1, D), lambda i: (i // nS, 0, 0)),
                  wspec(wa), wspec(wb), wspec(wc), wspec(wo)],
        out_specs=pl.BlockSpec((1, tm, D), rmap),
        out_shape=jax.ShapeDtypeStruct((B, S, D), F32),
        compiler_params=_cparams(("arbitrary",)),
        name="merge_out",
    )(x, ya, yb, yc, proj, proj, proj, gmod.reshape(B, 1, D), wa, wb, wc, wo)


def _permute_w_in(w_in, D):
    w_in = w_in.astype(BF16)
    o = 0
    aq = w_in[..., o:o + 3 * GDN_WIDTH]; o += 3 * GDN_WIDTH
    az = w_in[..., o:o + GDN_WIDTH]; o += GDN_WIDTH
    ba = w_in[..., o:o + 2 * GDN_HEADS]; o += 2 * GDN_HEADS
    bu = w_in[..., o:o + S5_WIDTH]; o += S5_WIDTH
    cq = w_in[..., o:o + 3 * DIL_WIDTH]; o += 3 * DIL_WIDTH
    gt = w_in[..., o:o + N_BRANCH * D]
    used = w_in.shape[-1]
    total = -(-used // PROJ_COL_TILE) * PROJ_COL_TILE
    pad = jnp.zeros(w_in.shape[:-1] + (total - used,), w_in.dtype)
    return jnp.concatenate([bu, cq, az, gt, aq, ba, pad], axis=-1).astype(BF16)


def kernel(x, c, ada_w, ada_b, norm_ffn1, ffn1_w1, ffn1_w3, ffn1_w2, norm_mix, w_in, gdn_conv, gdn_a_log, gdn_dt_bias, gdn_out_norm, s5_a_re, s5_a_im, s5_b_re, s5_b_im, s5_c_re, s5_c_im, s5_d, s5_log_step, s5_glu_w, s5_glu_b, dil_q_norm, dil_k_norm, w_branch_a, w_branch_b, w_branch_c, w_out, norm_ffn2, ffn2_w1, ffn2_w3, ffn2_w2):
    B, S, D = x.shape
    L = ada_w.shape[0]
    off_aq = OFF_GATES + N_BRANCH * D
    off_ba = off_aq + 3 * GDN_WIDTH
    mods = _ada_mod(c, ada_w, ada_b)
    f1 = [w.astype(BF16) for w in (ffn1_w1, ffn1_w3, ffn1_w2)]
    f2 = [w.astype(BF16) for w in (ffn2_w1, ffn2_w3, ffn2_w2)]
    w_proj = _permute_w_in(w_in, D)
    for l in range(L):
        sh1, sc1, g1, sh2, sc2, g2, sh3, sc3, g3 = [mods[l, :, i * D:(i + 1) * D] for i in range(N_MOD)]
        x = _ffn(x, norm_ffn1[l], sh1, sc1, g1, *f1, l)
        proj = _proj(x, norm_mix[l], sh2, sc2, w_proj, l)
        ya = _gdn(proj, gdn_conv[l], gdn_a_log[l], gdn_dt_bias[l], gdn_out_norm[l], off_aq, off_ba)
        yb = _s5(proj, s5_a_re[l], s5_a_im[l], s5_b_re[l], s5_b_im[l], s5_c_re[l], s5_c_im[l],
                 s5_d[l], s5_log_step[l], s5_glu_w[l], s5_glu_b[l])
        yc = _dil(proj, dil_q_norm[l], dil_k_norm[l])
        x = _merge(x, ya, yb, yc, proj, g2, w_branch_a[l].astype(BF16), w_branch_b[l].astype(BF16),
                   w_branch_c[l].astype(BF16), w_out[l].astype(BF16))
        x = _ffn(x, norm_ffn2[l], sh3, sc3, g3, *f2, l)
    return x
```

```python
import functools
import math

import jax
import jax.numpy as jnp
from jax import lax
from jax.experimental import pallas as pl
from jax.experimental.pallas import tpu as pltpu

F32 = jnp.float32
BF16 = jnp.bfloat16
EPS = 1e-6
NEG = -1e30

LANES = 128
V7X_VMEM_LIMIT_BYTES = 56 * 1024 * 1024

FFN_RES = 0.5
FFN_OUT_CHUNK = 512
FFN_NORM_ROWS = 256
PROJ_COL_TILE = 2304
N_MOD = 9
GDN_HEADS = 8
GDN_HEAD_DIM = 128
GDN_WIDTH = GDN_HEADS * GDN_HEAD_DIM
GDN_CONV = 4
GDN_CHUNK = 64
GDN_PHASE1_CHUNKS = 16
GDN_CONV_ROWS = 256
S5_GROUP = 16
S5_WIDTH = 768
S5_GROUPS = S5_WIDTH // S5_GROUP
S5_STATE = 64
S5_MAX_RE = -1e-4
S5_TILE_GROUPS = LANES // S5_GROUP
S5_TILES = S5_GROUPS // S5_TILE_GROUPS
S5_TILE_STATES = S5_TILE_GROUPS * S5_STATE
S5_SCAN_UNROLL = 4
DIL_PAIRS = ((128, 1), (512, 4), (2048, 16))
DIL_HEADS_PER_GROUP = 4
DIL_HEAD_DIM = 64
DIL_SUBHEADS = len(DIL_PAIRS) * DIL_HEADS_PER_GROUP
DIL_WIDTH = DIL_SUBHEADS * DIL_HEAD_DIM
DIL_OUT = DIL_HEADS_PER_GROUP * DIL_HEAD_DIM
DIL_BLK = 128
ALIBI_MAX = 8.0
N_BRANCH = 3

OFF_S5 = 0
OFF_CQ = OFF_S5 + S5_WIDTH
OFF_CK = OFF_CQ + DIL_WIDTH
OFF_CV = OFF_CK + DIL_WIDTH
OFF_Z = OFF_CV + DIL_WIDTH
OFF_GATES = OFF_Z + GDN_WIDTH
OFF_AQ = None


def _cparams(sem, vmem=V7X_VMEM_LIMIT_BYTES):
    return pltpu.CompilerParams(dimension_semantics=sem, vmem_limit_bytes=vmem)


def _mxu(x):
    return x if x.dtype == BF16 else x.astype(BF16)


def _dot(a, b):
    return jnp.dot(_mxu(a), _mxu(b), preferred_element_type=F32)


def _dot_nt(a, b):
    return lax.dot_general(_mxu(a), _mxu(b), (((1,), (1,)), ((), ())), preferred_element_type=F32)


def _dot_tn(a, b):
    return lax.dot_general(_mxu(a), _mxu(b), (((0,), (0,)), ((), ())), preferred_element_type=F32)


def _silu(x):
    hx = 0.5 * x
    return hx * jnp.tanh(hx) + hx


def _ada_kernel(c_ref, w_ref, b_ref, o_ref):
    c = c_ref[...]
    o_ref[0] = _dot(_silu(c), w_ref[0]) + b_ref[0]


def _ada_mod(c, ada_w, ada_b, tn=1024):
    L, D, N = ada_w.shape
    B = c.shape[0]
    return pl.pallas_call(
        _ada_kernel,
        grid=(L, N // tn),
        in_specs=[pl.BlockSpec((B, D), lambda l, j: (0, 0)),
                  pl.BlockSpec((1, D, tn), lambda l, j: (l, 0, j)),
                  pl.BlockSpec((1, 1, tn), lambda l, j: (l, 0, j))],
        out_specs=pl.BlockSpec((1, B, tn), lambda l, j: (l, 0, j)),
        out_shape=jax.ShapeDtypeStruct((L, B, N), F32),
        compiler_params=_cparams(("arbitrary", "arbitrary")),
        name="ada_mod",
    )(c, ada_w, ada_b.reshape(L, 1, N))


def _norm_mod(x, g, sh, sc):
    ms = jnp.mean(x * x, axis=-1, keepdims=True)
    return (x * lax.rsqrt(ms + EPS)) * (g * (1.0 + sc)) + sh


def _ffn_kernel(x_ref, g_ref, sh_ref, sc_ref, gate_ref, w1_ref, w3_ref, w2_ref, o_ref, hn_ref):
    j = pl.program_id(1)

    @pl.when(j == 0)
    def _():
        rc = FFN_NORM_ROWS
        for r in range(x_ref.shape[1] // rc):
            rs = slice(r * rc, (r + 1) * rc)
            x = x_ref[0, rs, :]
            hn_ref[rs, :] = _norm_mod(x, g_ref[...], sh_ref[0], sc_ref[0]).astype(BF16)
            o_ref[0, rs, :] = x

    hn = hn_ref[...]
    h1 = _dot(hn, w1_ref[...])
    h3 = _dot(hn, w3_ref[...])
    a = (_silu(h1) * h3).astype(BF16)
    scale = FFN_RES * gate_ref[0]
    tn = FFN_OUT_CHUNK
    for n in range(o_ref.shape[-1] // tn):
        cs = slice(n * tn, (n + 1) * tn)
        o_ref[0, :, cs] += scale[:, cs] * _dot(a, w2_ref[:, cs])


def _ffn(x, g, sh, sc, gate, w1, w3, w2, l, tm=1024, tf=512):
    B, S, D = x.shape
    F = w1.shape[2]
    nS = S // tm
    xmap = lambda i, j: (i // nS, i % nS, 0)
    bmap = lambda i, j: (i // nS, 0, 0)
    return pl.pallas_call(
        _ffn_kernel,
        grid=(B * nS, F // tf),
        in_specs=[pl.BlockSpec((1, tm, D), xmap),
                  pl.BlockSpec((1, D), lambda i, j: (0, 0)),
                  pl.BlockSpec((1, 1, D), bmap),
                  pl.BlockSpec((1, 1, D), bmap),
                  pl.BlockSpec((1, 1, D), bmap),
                  pl.BlockSpec((None, D, tf), lambda i, j: (l, 0, j)),
                  pl.BlockSpec((None, D, tf), lambda i, j: (l, 0, j)),
                  pl.BlockSpec((None, tf, D), lambda i, j: (l, j, 0))],
        out_specs=pl.BlockSpec((1, tm, D), xmap),
        out_shape=jax.ShapeDtypeStruct((B, S, D), F32),
        scratch_shapes=[pltpu.VMEM((tm, D), BF16)],
        compiler_params=_cparams(("arbitrary", "arbitrary")),
        name="ffn",
    )(x, g.reshape(1, D), sh.reshape(B, 1, D), sc.reshape(B, 1, D), gate.reshape(B, 1, D),
      w1, w3, w2)


def _proj_kernel(x_ref, g_ref, sh_ref, sc_ref, w_ref, o_ref, hn_ref):
    @pl.when(pl.program_id(1) == 0)
    def _():
        rc = FFN_NORM_ROWS
        for r in range(x_ref.shape[1] // rc):
            rs = slice(r * rc, (r + 1) * rc)
            hn_ref[rs, :] = _norm_mod(x_ref[0, rs, :], g_ref[...], sh_ref[0], sc_ref[0]).astype(BF16)

    o_ref[0] = _dot(hn_ref[...], w_ref[...])


def _proj(x, g, sh, sc, w, l, tm=1024, tn=PROJ_COL_TILE):
    B, S, D = x.shape
    N = w.shape[2]
    nS = S // tm
    xmap = lambda i, j: (i // nS, i % nS, 0)
    bmap = lambda i, j: (i // nS, 0, 0)
    return pl.pallas_call(
        _proj_kernel,
        grid=(B * nS, N // tn),
        in_specs=[pl.BlockSpec((1, tm, D), xmap, pipeline_mode=pl.Buffered(1)),
                  pl.BlockSpec((1, D), lambda i, j: (0, 0)),
                  pl.BlockSpec((1, 1, D), bmap),
                  pl.BlockSpec((1, 1, D), bmap),
                  pl.BlockSpec((None, D, tn), lambda i, j: (l, 0, j))],
        out_specs=pl.BlockSpec((1, tm, tn), lambda i, j: (i // nS, i % nS, j)),
        out_shape=jax.ShapeDtypeStruct((B, S, N), F32),
        scratch_shapes=[pltpu.VMEM((tm, D), BF16)],
        compiler_params=_cparams(("arbitrary", "arbitrary")),
        name="in_proj",
    )(x, g.reshape(1, D), sh.reshape(B, 1, D), sc.reshape(B, 1, D), w)


def _gdn_kernel(q_ref, k_ref, v_ref, z_ref, ba_ref, cq_ref, ck_ref, cv_ref, al_ref, dtb_ref,
                on_ref, o_ref, qn_s, kn_s, vv_s, qo_s, kp_s, cols_s, gct_s,
                *, S, HG):
    C = GDN_CHUNK
    HD = GDN_HEAD_DIM
    hg = pl.program_id(1)
    NH = GDN_HEADS
    shift = (LANES - hg * HG) % LANES
    t16 = pltpu.roll(ba_ref[0], shift, 1).T[0:2 * NH, :]
    a_col = al_ref[0][:, 0:1]
    d_col = dtb_ref[0][:, 0:1]
    beta16 = jax.nn.sigmoid(t16)
    xg = t16 + d_col
    g16 = -jnp.exp(a_col) * (jnp.maximum(xg, 0.0) + jnp.log1p(jnp.exp(-jnp.abs(xg))))
    lanec = lax.broadcasted_iota(jnp.int32, (2 * NH, S), 1) % C
    for d in (1, 2, 4, 8, 16, 32):
        g16 = g16 + jnp.where(lanec >= d, pltpu.roll(g16, d, 1), 0.0)
    gl16 = jnp.where(lanec == C - 1, g16, 0.0)
    for d in (1, 2, 4, 8, 16, 32):
        gl16 = gl16 + jnp.where(lanec + d < C, pltpu.roll(gl16, S - d, 1), 0.0)
    gct_s[...] = g16[NH:2 * NH, :]
    cols_s[...] = jnp.concatenate(
        [beta16[0:NH], g16[NH:2 * NH], gl16[NH:2 * NH], jnp.zeros((LANES - 3 * NH, S), F32)], axis=0).T

    SUB = 8
    RB = GDN_CONV_ROWS

    def l2n(x, scale=1.0):
        return x * (lax.rsqrt(jnp.sum(x * x, axis=-1, keepdims=True) + EPS) * scale)

    def conv_block(rb, carry):
        r0 = pl.multiple_of(rb * RB, RB)
        rp = pl.multiple_of(jnp.maximum(r0 - SUB, 0), SUB)
        keep = jnp.where(rb > 0, 1.0, 0.0)
        rows = pl.ds(r0, RB)
        K = GDN_CONV
        for j in range(HG):
            sl = slice(j * HD, (j + 1) * HD)
            for x_ref, w_ref, dst, post in ((q_ref, cq_ref, qn_s, lambda y: l2n(y, HD ** -0.5)),
                                            (k_ref, ck_ref, kn_s, l2n),
                                            (v_ref, cv_ref, vv_s, lambda y: y)):
                w = w_ref[:, sl]
                cur = x_ref[0, rows, sl]
                hist = jnp.concatenate([x_ref[0, pl.ds(rp, SUB), sl] * keep, cur], axis=0)
                acc = cur * w[K - 1:K, :]
                for kk in range(K - 1):
                    acc = acc + pltpu.roll(hist, K - 1 - kk, 0)[SUB:] * w[kk:kk + 1, :]
                dst[rows, sl] = post(_silu(acc))
        return carry

    lax.fori_loop(0, S // RB, conv_block, 0)

    ii = lax.broadcasted_iota(jnp.int32, (C, C), 0)
    jj = lax.broadcasted_iota(jnp.int32, (C, C), 1)
    tril = ii >= jj
    strict = ii > jj
    eye = jnp.where(ii == jj, 1.0, 0.0)
    bd = {w: (ii // w) == (jj // w) for w in (8, 16, 32)}
    base_mask = strict & bd[8]
    merge_masks = (bd[16] & ~bd[8], bd[32] & ~bd[16], ~bd[32])
    NC = GDN_PHASE1_CHUNKS

    def phase1(p, carry):
        r0 = pl.multiple_of(p * NC * C, NC * C)
        gct = gct_s[:, pl.ds(r0, NC * C)]
        chains = []
        for cc in range(NC):
            rows = pl.ds(r0 + cc * C, C)
            cols = cols_s[rows, :]
            for j in range(HG):
                sl = slice(j * HD, (j + 1) * HD)
                gcol = cols[:, NH + j:NH + j + 1]
                glcol = cols[:, 2 * NH + j:2 * NH + j + 1]
                grow = gct[j:j + 1, cc * C:(cc + 1) * C]
                bcol = cols[:, j:j + 1]
                kn_c = kn_s[rows, sl]
                kb = kn_c * bcol
                eg = jnp.exp(gcol)
                chains.append(dict(
                    cc=cc, j=j, rows=rows, kn=kn_c, kb=kb, qn=qn_s[rows, sl], eg=eg,
                    egl=jnp.exp(glcol[0:1, :]),
                    decay=jnp.where(tril, jnp.exp(jnp.minimum(gcol - grow, 0.0)), 0.0),
                    rhs=jnp.concatenate([vv_s[rows, sl] * bcol, kb * eg], axis=1),
                    kdec=kn_c * jnp.exp(glcol - gcol)))
        for ch in chains:
            ch['lm'] = jnp.where(strict, _dot_nt(ch['kb'], ch['kn']) * ch['decay'], 0.0)
        for ch in chains:
            ch['attn'] = _dot_nt(ch['qn'], ch['kn']) * ch['decay']
        for ch in chains:
            ch['l8'] = jnp.where(base_mask, ch['lm'], 0.0)
            ch['s2'] = _dot(ch['l8'], ch['l8'])
        for ch in chains:
            n1 = eye - ch['l8']
            ch['n2'] = n1 + _dot(n1, ch['s2'])
        for ch in chains:
            ch['s4'] = _dot(ch['s2'], ch['s2'])
        for ch in chains:
            ch['t'] = ch['n2'] + _dot(ch['n2'], ch['s4'])
        for mm in merge_masks:
            for ch in chains:
                ch['tl'] = _dot(ch['t'], jnp.where(mm, ch['lm'], 0.0))
            for ch in chains:
                ch['t'] = ch['t'] - _dot(ch['tl'], ch['t'])
        for ch in chains:
            ch['x'] = _dot(ch['t'], ch['rhs'])
        for ch in chains:
            au = _dot(ch['attn'], ch['x'])
            ch['oq'] = (au[:, :HD], ch['qn'] * ch['eg'] - au[:, HD:])
        for ch in chains:
            ch['ku'] = _dot_tn(ch['kdec'], ch['x'])
        pairs = [(chains[cp * HG + j], chains[(cp + 1) * HG + j])
                 for cp in range(0, NC, 2) for j in range(HG)]
        for ch0, ch1 in pairs:
            j = ch0['j']
            o1, q1 = ch1['oq']
            pr = _dot(jnp.concatenate([ch1['ku'][:, HD:], q1], axis=0), ch0['ku'])
            e0, e1 = ch0['egl'], ch1['egl']
            b2 = e1 * ch0['ku'][:, :HD] - pr[:HD, :HD] + ch1['ku'][:, :HD]
            m2 = e1 * ch0['ku'][:, HD:] + e0 * ch1['ku'][:, HD:] - pr[:HD, HD:]
            kp_s[p * (NC // 2) + ch0['cc'] // 2, j] = jnp.concatenate([b2, m2], axis=1)
            cs = slice(j * 2 * HD, (j + 1) * 2 * HD)
            qo_s[ch0['rows'], cs] = jnp.concatenate(ch0['oq'], axis=1)
            qo_s[ch1['rows'], cs] = jnp.concatenate([pr[HD:, :HD] + o1, e0 * q1 - pr[HD:, HD:]], axis=1)
        return carry

    lax.fori_loop(0, S // (NC * C), phase1, 0)

    def gated_out(o, rows, sl):
        on = o * lax.rsqrt(jnp.mean(o * o, axis=-1, keepdims=True) + EPS) * on_ref[...]
        zc = z_ref[0, rows, sl]
        o_ref[0, rows, sl] = (on * _silu(zc)).astype(o_ref.dtype)

    def phase2(pi, hs):
        r0 = pl.multiple_of(pi * 2 * C, 2 * C)
        rows2 = pl.ds(r0, 2 * C)
        gla = cols_s[pl.ds(r0, SUB), :]
        glb = cols_s[pl.ds(r0 + C, SUB), :]
        st = []
        for j in range(HG):
            kp = kp_s[pi, j]
            qo = qo_s[rows2, j * 2 * HD:(j + 1) * 2 * HD]
            st.append((kp, qo, _dot(jnp.concatenate([kp[:, HD:], qo[:, HD:]], axis=0), hs[j])))
        new_hs = []
        for j in range(HG):
            kp, qo, r = st[j]
            e01 = jnp.exp(gla[0:1, 2 * NH + j:2 * NH + j + 1] + glb[0:1, 2 * NH + j:2 * NH + j + 1])
            new_hs.append(hs[j] * e01 - r[:HD] + kp[:, :HD])
            gated_out(r[HD:] + qo[:, :HD], rows2, slice(j * HD, (j + 1) * HD))
        return tuple(new_hs)

    h0 = tuple(jnp.zeros((HD, HD), F32) for _ in range(HG))
    lax.fori_loop(0, S // (2 * C), phase2, h0)


def _gdn(proj, conv_w, a_log, dt_bias, out_norm, off_aq, off_ba, HG=2):
    B, S, _ = proj.shape
    W = HG * GDN_HEAD_DIM
    nG = GDN_HEADS // HG

    def pad(v):
        rows = jnp.zeros((nG, 2 * GDN_HEADS), F32).at[:, GDN_HEADS:GDN_HEADS + HG].set(v.reshape(nG, HG))
        return jnp.broadcast_to(rows[:, :, None], (nG, 2 * GDN_HEADS, LANES))

    kern = functools.partial(_gdn_kernel, S=S, HG=HG)
    par = pl.BlockSpec((1, 2 * GDN_HEADS, LANES), lambda b, h: (h, 0, 0))
    col = lambda base: (lambda b, h: (b, 0, base // W + h))
    cw = lambda base: (lambda b, h: (0, base // W + h))
    one = lambda b, h: (0, 0)
    return pl.pallas_call(
        kern,
        grid=(B, nG),
        in_specs=[pl.BlockSpec((1, S, W), col(off_aq)),
                  pl.BlockSpec((1, S, W), col(off_aq + GDN_WIDTH)),
                  pl.BlockSpec((1, S, W), col(off_aq + 2 * GDN_WIDTH)),
                  pl.BlockSpec((1, S, W), col(OFF_Z)),
                  pl.BlockSpec((1, S, LANES), lambda b, h: (b, 0, off_ba // LANES)),
                  pl.BlockSpec((GDN_CONV, W), cw(0)),
                  pl.BlockSpec((GDN_CONV, W), cw(GDN_WIDTH)),
                  pl.BlockSpec((GDN_CONV, W), cw(2 * GDN_WIDTH)),
                  par, par,
                  pl.BlockSpec((1, GDN_HEAD_DIM), one)],
        out_specs=pl.BlockSpec((1, S, W), lambda b, h: (b, 0, h)),
        out_shape=jax.ShapeDtypeStruct((B, S, GDN_WIDTH), BF16),
        scratch_shapes=[pltpu.VMEM((S, W), F32), pltpu.VMEM((S, W), F32), pltpu.VMEM((S, W), F32),
                        pltpu.VMEM((S, 2 * W), F32),
                        pltpu.VMEM((S // (2 * GDN_CHUNK), HG, GDN_HEAD_DIM, 2 * GDN_HEAD_DIM), F32),
                        pltpu.VMEM((S, LANES), F32), pltpu.VMEM((GDN_HEADS, S), F32)],
        compiler_params=_cparams(("arbitrary", "arbitrary")),
        name="gdn",
    )(proj, proj, proj, proj, proj, conv_w, conv_w, conv_w, pad(a_log), pad(dt_bias),
      out_norm.reshape(1, GDN_HEAD_DIM))


def _s5_disc_kernel(are_ref, aim_ref, ls_ref, bre_ref, bim_ref, lr_ref, li_ref, br_ref, bi_ref):
    lam_re = jnp.minimum(are_ref[...], S5_MAX_RE)
    lam_im = aim_ref[...]
    step = jnp.exp(ls_ref[...])
    mag = jnp.exp(lam_re * step)
    lbr = mag * jnp.cos(lam_im * step)
    lbi = mag * jnp.sin(lam_im * step)
    lr_ref[...] = lbr
    li_ref[...] = lbi
    nr = lbr - 1.0
    den = lam_re * lam_re + lam_im * lam_im
    cr = (nr * lam_re + lbi * lam_im) / den
    ci = (lbi * lam_re - nr * lam_im) / den
    bre = bre_ref[...]
    bim = bim_ref[...]
    br_ref[...] = cr * bre - ci * bim
    bi_ref[...] = cr * bim + ci * bre


def _s5_discretize(a_re, a_im, log_step, b_re, b_im):
    G, P = a_re.shape
    I = b_re.shape[-1]
    gp = jax.ShapeDtypeStruct((G, 1, P), F32)
    gip = jax.ShapeDtypeStruct((G, I, P), F32)
    return pl.pallas_call(
        _s5_disc_kernel, out_shape=(gp, gp, gip, gip), name="s5_discretize",
    )(a_re.reshape(G, 1, P), a_im.reshape(G, 1, P), log_step.reshape(G, 1, 1),
      jnp.swapaxes(b_re, 1, 2), jnp.swapaxes(b_im, 1, 2))


def _s5_kernel(u_ref, pf_ref, pb_ref, bb_ref, lr_ref, li_ref, cc_ref, d_ref, gw_ref, gb_ref,
               o_ref, xr_s, xi_s, cr_s, ci_s, y_s, *, B, tt):
    TS = S5_TILE_STATES
    R = B * tt

    @pl.when(pl.program_id(0) == 0)
    def _():
        cr_s[...] = jnp.zeros_like(cr_s)
        ci_s[...] = jnp.zeros_like(ci_s)

    u = u_ref[...].reshape(R, S5_WIDTH)
    u_hi = u.astype(BF16)
    u_lo = (u - u_hi.astype(F32)).astype(BF16)
    up = _dot(pf_ref[...], jnp.concatenate([u_hi, u_lo], axis=1))
    u_tb = up[:, :S5_WIDTH] + up[:, S5_WIDTH:]
    for t in range(S5_TILES):
        ut = up[:, t * LANES:(t + 1) * LANES].astype(BF16)
        bu = _dot(ut, bb_ref[t])
        xr_s[:, :, t * TS:(t + 1) * TS] = bu[:, :TS].reshape(tt, B, TS)
        xi_s[:, :, t * TS:(t + 1) * TS] = bu[:, TS:].reshape(tt, B, TS)

    for t in range(S5_TILES):
        ls = slice(t * TS, (t + 1) * TS)
        ar = lr_ref[:, ls]
        ai = li_ref[:, ls]

        def step(s, carry, ls=ls, ar=ar, ai=ai):
            xr, xi = carry
            nr = ar * xr - ai * xi + xr_s[s, :, ls]
            ni = ar * xi + ai * xr + xi_s[s, :, ls]
            xr_s[s, :, ls] = nr
            xi_s[s, :, ls] = ni
            return nr, ni

        xr, xi = lax.fori_loop(0, tt, step, (cr_s[:, ls], ci_s[:, ls]), unroll=S5_SCAN_UNROLL)
        cr_s[:, ls] = xr
        ci_s[:, ls] = xi

    for t in range(S5_TILES):
        ls = slice(t * TS, (t + 1) * TS)
        lo = slice(t * LANES, (t + 1) * LANES)
        x = jnp.concatenate([xr_s[:, :, ls].reshape(R, TS), xi_s[:, :, ls].reshape(R, TS)], axis=1)
        y = _dot(x, cc_ref[t]) + d_ref[:, lo] * u_tb[:, lo]
        y_s[:, lo] = jax.nn.gelu(y).astype(BF16)
    z = _dot(y_s[...], gw_ref[...]) + gb_ref[...]
    out = (z[:, :S5_WIDTH] * jax.nn.sigmoid(z[:, S5_WIDTH:])).astype(BF16)
    o_ref[...] = _dot(pb_ref[...], out).reshape(B, tt, S5_WIDTH).astype(o_ref.dtype)


def _blockdiag_in(m):
    t = m.reshape(S5_TILES, S5_TILE_GROUPS, S5_GROUP, S5_STATE)
    eye = jnp.eye(S5_TILE_GROUPS, dtype=m.dtype)
    return jnp.einsum('tgip,gh->tgihp', t, eye).reshape(S5_TILES, LANES, S5_TILE_STATES)


def _blockdiag_out(m):
    t = m.reshape(S5_TILES, S5_TILE_GROUPS, S5_GROUP, S5_STATE)
    eye = jnp.eye(S5_TILE_GROUPS, dtype=m.dtype)
    return jnp.einsum('tgip,gh->tgphi', t, eye).reshape(S5_TILES, S5_TILE_STATES, LANES)


def _s5(proj, a_re, a_im, b_re, b_im, c_re, c_im, d_skip, log_step, glu_w, glu_b, tt=32):
    B, S, _ = proj.shape
    lr, li, bbr, bbi = _s5_discretize(a_re, a_im, log_step, b_re, b_im)
    NS = S5_GROUPS * S5_STATE
    kern = functools.partial(_s5_kernel, B=B, tt=tt)
    R = B * tt
    src = (jnp.arange(R) % B) * tt + jnp.arange(R) // B
    pf = (src[:, None] == jnp.arange(R)[None, :]).astype(BF16)
    bb = jnp.concatenate([_blockdiag_in(bbr), _blockdiag_in(bbi)], axis=2).astype(BF16)
    cc = jnp.concatenate([_blockdiag_out(c_re), -_blockdiag_out(c_im)], axis=1).astype(BF16)
    full2 = lambda i: (0, 0)
    full3 = lambda i: (0, 0, 0)
    return pl.pallas_call(
        kern,
        grid=(S // tt,),
        in_specs=[pl.BlockSpec((B, tt, S5_WIDTH), lambda i: (0, i, OFF_S5 // S5_WIDTH)),
                  pl.BlockSpec((R, R), full2),
                  pl.BlockSpec((R, R), full2),
                  pl.BlockSpec((S5_TILES, LANES, 2 * S5_TILE_STATES), full3),
                  pl.BlockSpec((1, NS), full2),
                  pl.BlockSpec((1, NS), full2),
                  pl.BlockSpec((S5_TILES, 2 * S5_TILE_STATES, LANES), full3),
                  pl.BlockSpec((1, S5_WIDTH), full2),
                  pl.BlockSpec((S5_WIDTH, 2 * S5_WIDTH), full2),
                  pl.BlockSpec((1, 2 * S5_WIDTH), full2)],
        out_specs=pl.BlockSpec((B, tt, S5_WIDTH), lambda i: (0, i, 0)),
        out_shape=jax.ShapeDtypeStruct((B, S, S5_WIDTH), BF16),
        scratch_shapes=[pltpu.VMEM((tt, B, NS), F32), pltpu.VMEM((tt, B, NS), F32),
                        pltpu.VMEM((B, NS), F32), pltpu.VMEM((B, NS), F32),
                        pltpu.VMEM((R, S5_WIDTH), BF16)],
        compiler_params=_cparams(("arbitrary",)),
        name="s5",
    )(proj, pf, pf.T, bb, lr.reshape(1, NS), li.reshape(1, NS), cc,
      d_skip.reshape(1, S5_WIDTH), glu_w.astype(BF16), glu_b.reshape(1, 2 * S5_WIDTH))


def _dil_kernel(q0, q1, q2, k0, k1, k2, v0, v1, v2, qg_ref, kg_ref, sl_ref, o_ref,
                qs, ks, vs, og, mg, lg, oacc, macc, lacc, *, S):
    q_refs, k_refs, v_refs = (q0, q1, q2), (k0, k1, k2), (v0, v1, v2)
    BLK = DIL_BLK
    HD = DIL_HEAD_DIM
    lane = lax.broadcasted_iota(jnp.int32, (1, LANES), 1)
    first = lane < HD
    ii = lax.broadcasted_iota(jnp.int32, (BLK, BLK), 0)
    jj = lax.broadcasted_iota(jnp.int32, (BLK, BLK), 1)
    steps_cur = (ii - jj).astype(F32)
    steps_prev = (ii - jj + BLK).astype(F32)
    ok_cur = ii >= jj
    ok_prev = jj >= ii

    li = lax.broadcasted_iota(jnp.int32, (LANES, LANES), 0) // HD
    lj = lax.broadcasted_iota(jnp.int32, (LANES, LANES), 1) // HD
    avg = jnp.where(li == lj, 1.0 / HD, 0.0).astype(BF16)

    def rms_pair(x, gain):
        x2 = x * x
        hi = x2.astype(BF16)
        lo = (x2 - hi.astype(F32)).astype(BF16)
        ms = _dot(hi, avg) + _dot(lo, avg)
        return x * lax.rsqrt(ms + EPS) * gain

    for g, (window, d) in enumerate(DIL_PAIRS):
        sd = S // d
        nb = sd // BLK
        for r in range(d):
            src = pl.ds(r, sd, stride=d) if d > 1 else slice(None)
            dst = slice(r * sd, (r + 1) * sd)
            qs[dst, :] = rms_pair(q_refs[g][0, src, :], qg_ref[...]) * (HD ** -0.5)
            ks[dst, :] = rms_pair(k_refs[g][0, src, :], kg_ref[...])
            vs[dst, :] = v_refs[g][0, src, :]

        sl_a = sl_ref[0, g:g + 1, 0:1] * float(d)
        sl_b = sl_ref[0, g:g + 1, HD:HD + 1] * float(d)
        bias = [(jnp.where(ok_cur, -s * steps_cur, NEG), jnp.where(ok_prev, -s * steps_prev, NEG))
                for s in (sl_a, sl_b)]

        def block_pair(ip, carry, nb=nb, bias=bias):
            st0 = pl.multiple_of(ip * 2 * BLK, 2 * BLK)
            ch = []
            for bi in range(2):
                rows = pl.ds(st0 + bi * BLK, BLK)
                blk = dict(rows=rows, q=qs[rows, :], kc=ks[rows, :], vc=vs[rows, :])
                if nb > 1 and bi == 0:
                    stp = pl.multiple_of(jnp.maximum(st0 - BLK, 0), BLK)
                    blk.update(kp=ks[pl.ds(stp, BLK), :], vp=vs[pl.ds(stp, BLK), :],
                               pen=jnp.where(((2 * ip) % nb) > 0, 0.0, NEG))
                elif nb > 1:
                    blk.update(kp=ch[0]['blk']['kc'], vp=ch[0]['blk']['vc'], pen=0.0)
                for hh in range(2):
                    qh = jnp.where(first, blk['q'], 0.0) if hh == 0 else jnp.where(first, 0.0, blk['q'])
                    ch.append(dict(blk=blk, hh=hh, qh=qh))
            for c in ch:
                c['sc'] = _dot_nt(c['qh'], c['blk']['kc']) + bias[c['hh']][0]
            if nb > 1:
                for c in ch:
                    c['sp'] = _dot_nt(c['qh'], c['blk']['kp']) + (bias[c['hh']][1] + c['blk']['pen'])
            for c in ch:
                m = jnp.max(c['sc'], axis=-1, keepdims=True)
                if nb > 1:
                    m = jnp.maximum(m, jnp.max(c['sp'], axis=-1, keepdims=True))
                c['m'] = m
                c['pc'] = jnp.exp(c['sc'] - m)
                c['l'] = jnp.sum(c['pc'], axis=-1, keepdims=True)
                if nb > 1:
                    c['pp'] = jnp.exp(c['sp'] - m)
                    c['l'] = c['l'] + jnp.sum(c['pp'], axis=-1, keepdims=True)
            for c in ch:
                c['o'] = _dot(c['pc'], c['blk']['vc'])
            if nb > 1:
                for c in ch:
                    c['o'] = c['o'] + _dot(c['pp'], c['blk']['vp'])
            for bi in range(2):
                a, b = ch[2 * bi], ch[2 * bi + 1]
                rows = a['blk']['rows']
                og[rows, :] = jnp.where(first, a['o'], b['o'])
                mg[rows, :] = jnp.where(first, a['m'], b['m'])
                lg[rows, :] = jnp.where(first, a['l'], b['l'])
            return carry

        lax.fori_loop(0, S // (2 * BLK), block_pair, 0)

        if g == 0:
            oacc[...] = og[...]
            macc[...] = mg[...]
            lacc[...] = lg[...]
        else:
            for r in range(d):
                tok = pl.ds(r, sd, stride=d)
                res = slice(r * sd, (r + 1) * sd)
                m_old = macc[tok, :]
                m_g = mg[res, :]
                m_new = jnp.maximum(m_old, m_g)
                a = jnp.exp(m_old - m_new)
                b = jnp.exp(m_g - m_new)
                lacc[tok, :] = a * lacc[tok, :] + b * lg[res, :]
                oacc[tok, :] = a * oacc[tok, :] + b * og[res, :]
                macc[tok, :] = m_new

    o_ref[0] = (oacc[...] / lacc[...]).astype(o_ref.dtype)


def _dil(proj, q_norm, k_norm):
    B, S, _ = proj.shape
    sub = jnp.arange(1, DIL_SUBHEADS + 1, dtype=F32)
    slopes = jnp.power(2.0, -ALIBI_MAX * sub / DIL_SUBHEADS)
    sl = slopes.reshape(len(DIL_PAIRS), 2, 2).transpose(1, 0, 2)
    sl = jnp.repeat(sl, DIL_HEAD_DIM, axis=2)
    tile2 = lambda v: jnp.tile(v.reshape(1, DIL_HEAD_DIM), (1, 2))
    kern = functools.partial(_dil_kernel, S=S)
    col = lambda base, g: (lambda b, hp: (b, 0, base // LANES + 2 * g + hp))
    specs = [pl.BlockSpec((1, S, LANES), col(base, g))
             for base in (OFF_CQ, OFF_CK, OFF_CV) for g in range(len(DIL_PAIRS))]
    one = lambda b, hp: (0, 0)
    seq = lambda: pltpu.VMEM((S, LANES), F32)
    return pl.pallas_call(
        kern,
        grid=(B, 2),
        in_specs=specs + [pl.BlockSpec((1, LANES), one), pl.BlockSpec((1, LANES), one),
                          pl.BlockSpec((1, len(DIL_PAIRS), LANES), lambda b, hp: (hp, 0, 0))],
        out_specs=pl.BlockSpec((1, S, LANES), lambda b, hp: (b, 0, hp)),
        out_shape=jax.ShapeDtypeStruct((B, S, DIL_OUT), BF16),
        scratch_shapes=[seq() for _ in range(9)],
        compiler_params=_cparams(("arbitrary", "arbitrary")),
        name="dilated_attn",
    )(*([proj] * 9), tile2(q_norm), tile2(k_norm), sl)


def _merge_kernel(x_ref, ya_ref, yb_ref, yc_ref, g0_ref, g1_ref, g2_ref, gm_ref,
                  wa_ref, wb_ref, wc_ref, wo_ref, o_ref):
    merged = (jax.nn.sigmoid(g0_ref[0]) * _dot(ya_ref[0], wa_ref[...])
              + jax.nn.sigmoid(g1_ref[0]) * _dot(yb_ref[0], wb_ref[...])
              + jax.nn.sigmoid(g2_ref[0]) * _dot(yc_ref[0], wc_ref[...]))
    out = _dot(merged.astype(BF16), wo_ref[...])
    o_ref[0] = x_ref[0] + gm_ref[0] * out


def _merge(x, ya, yb, yc, proj, gmod, wa, wb, wc, wo, tm=256):
    B, S, D = x.shape
    nS = S // tm
    rmap = lambda i: (i // nS, i % nS, 0)
    gate = lambda k: (lambda i: (i // nS, i % nS, OFF_GATES // D + k))
    const = lambda i: (0, 0)
    wspec = lambda w: pl.BlockSpec(w.shape, const, pipeline_mode=pl.Buffered(1))
    return pl.pallas_call(
        _merge_kernel,
        grid=(B * nS,),
        in_specs=[pl.BlockSpec((1, tm, D), rmap),
                  pl.BlockSpec((1, tm, ya.shape[-1]), rmap),
                  pl.BlockSpec((1, tm, yb.shape[-1]), rmap),
                  pl.BlockSpec((1, tm, yc.shape[-1]), rmap),
                  pl.BlockSpec((1, tm, D), gate(0)),
                  pl.BlockSpec((1, tm, D), gate(1)),
                  pl.BlockSpec((1, tm, D), gate(2)),
                  pl.BlockSpec((1, 1, D), lambda i: (i // nS, 0, 0)),
                  wspec(wa), wspec(wb), wspec(wc), wspec(wo)],
        out_specs=pl.BlockSpec((1, tm, D), rmap),
        out_shape=jax.ShapeDtypeStruct((B, S, D), F32),
        compiler_params=_cparams(("arbitrary",)),
        name="merge_out",
    )(x, ya, yb, yc, proj, proj, proj, gmod.reshape(B, 1, D), wa, wb, wc, wo)


def _permute_w_in(w_in, D):
    w_in = w_in.astype(BF16)
    o = 0
    aq = w_in[..., o:o + 3 * GDN_WIDTH]; o += 3 * GDN_WIDTH
    az = w_in[..., o:o + GDN_WIDTH]; o += GDN_WIDTH
    ba = w_in[..., o:o + 2 * GDN_HEADS]; o += 2 * GDN_HEADS
    bu = w_in[..., o:o + S5_WIDTH]; o += S5_WIDTH
    cq = w_in[..., o:o + 3 * DIL_WIDTH]; o += 3 * DIL_WIDTH
    gt = w_in[..., o:o + N_BRANCH * D]
    used = w_in.shape[-1]
    total = -(-used // PROJ_COL_TILE) * PROJ_COL_TILE
    pad = jnp.zeros(w_in.shape[:-1] + (total - used,), w_in.dtype)
    return jnp.concatenate([bu, cq, az, gt, aq, ba, pad], axis=-1).astype(BF16)


def kernel(x, c, ada_w, ada_b, norm_ffn1, ffn1_w1, ffn1_w3, ffn1_w2, norm_mix, w_in, gdn_conv, gdn_a_log, gdn_dt_bias, gdn_out_norm, s5_a_re, s5_a_im, s5_b_re, s5_b_im, s5_c_re, s5_c_im, s5_d, s5_log_step, s5_glu_w, s5_glu_b, dil_q_norm, dil_k_norm, w_branch_a, w_branch_b, w_branch_c, w_out, norm_ffn2, ffn2_w1, ffn2_w3, ffn2_w2):
    B, S, D = x.shape
    L = ada_w.shape[0]
    off_aq = OFF_GATES + N_BRANCH * D
    off_ba = off_aq + 3 * GDN_WIDTH
    mods = _ada_mod(c, ada_w, ada_b)
    f1 = [w.astype(BF16) for w in (ffn1_w1, ffn1_w3, ffn1_w2)]
    f2 = [w.astype(BF16) for w in (ffn2_w1, ffn2_w3, ffn2_w2)]
    w_proj = _permute_w_in(w_in, D)
    for l in range(L):
        sh1, sc1, g1, sh2, sc2, g2, sh3, sc3, g3 = [mods[l, :, i * D:(i + 1) * D] for i in range(N_MOD)]
        x = _ffn(x, norm_ffn1[l], sh1, sc1, g1, *f1, l)
        proj = _proj(x, norm_mix[l], sh2, sc2, w_proj, l)
        ya = _gdn(proj, gdn_conv[l], gdn_a_log[l], gdn_dt_bias[l], gdn_out_norm[l], off_aq, off_ba)
        yb = _s5(proj, s5_a_re[l], s5_a_im[l], s5_b_re[l], s5_b_im[l], s5_c_re[l], s5_c_im[l],
                 s5_d[l], s5_log_step[l], s5_glu_w[l], s5_glu_b[l])
        yc = _dil(proj, dil_q_norm[l], dil_k_norm[l])
        x = _merge(x, ya, yb, yc, proj, g2, w_branch_a[l].astype(BF16), w_branch_b[l].astype(BF16),
                   w_branch_c[l].astype(BF16), w_out[l].astype(BF16))
        x = _ffn(x, norm_ffn2[l], sh3, sc3, g3, *f2, l)
    return x
```
